```python
import math
import jax, jax.numpy as jnp
from jax import lax
import numpy as np

D_MODEL = 1024
BATCH = 8
SEQ = 2048
DEPTH = 2

PLE_DIM = 256
HEAD_DIM = 64
DA_HEADS = 4
DA_V = 2 * HEAD_DIM
DA_QK = DA_HEADS * 2 * HEAD_DIM
DA_WIDTH = DA_HEADS * DA_V
MLA_HEADS = 8
MLA_NOPE = 64
MLA_ROPE = 32
MLA_V = 64
MLA_Q_RANK = 256
MLA_KV_RANK = 128
MLA_WIDTH = MLA_HEADS * MLA_V
SW_HEADS = 8
SW_KV_HEADS = 2
SW_GROUP = SW_HEADS // SW_KV_HEADS
SW_WINDOW = 128
SW_WIDTH = SW_HEADS * HEAD_DIM
N_BRANCHES = 3
REL_BUCKETS = 32
REL_MAX_DIST = 128
BIAS_HEADS = DA_HEADS + SW_HEADS
Q_BLOCK = 128
ROPE_THETA = 10000.0
EPS = 1e-6
NEG = -1e30

IN_SIZES = (DA_QK, DA_QK, DA_WIDTH, DA_WIDTH,
            MLA_Q_RANK, MLA_KV_RANK, MLA_ROPE, MLA_WIDTH,
            SW_HEADS * HEAD_DIM, SW_KV_HEADS * HEAD_DIM, SW_KV_HEADS * HEAD_DIM, SW_WIDTH,
            N_BRANCHES * D_MODEL)
IN_TOTAL = sum(IN_SIZES)

kernel_name = "hybrid_diffattn_mla_swa_gated_merge"


def rms_norm(x, w):
    xf = x.astype(jnp.float32)
    y = xf * lax.rsqrt(jnp.mean(xf * xf, axis=-1, keepdims=True) + EPS)
    return (y * w.astype(jnp.float32)).astype(x.dtype)


def rope(x, positions):
    half = x.shape[-1] // 2
    inv_freq = ROPE_THETA ** (-jnp.arange(half, dtype=jnp.float32) / half)
    ang = positions.astype(jnp.float32)[..., None] * inv_freq
    ang = ang.reshape(ang.shape[:2] + (1,) * (x.ndim - 3) + (half,))
    cos, sin = jnp.cos(ang), jnp.sin(ang)
    xf = x.astype(jnp.float32)
    x1, x2 = xf[..., :half], xf[..., half:]
    return jnp.concatenate([x1 * cos - x2 * sin, x2 * cos + x1 * sin], axis=-1).astype(x.dtype)


def t5_bucket(rel):
    n = jnp.maximum(rel, 0)
    max_exact = REL_BUCKETS // 2
    nf = jnp.maximum(n, 1).astype(jnp.float32)
    large = max_exact + (jnp.log(nf / max_exact) / math.log(REL_MAX_DIST / max_exact)
                         * (REL_BUCKETS - max_exact)).astype(jnp.int32)
    large = jnp.minimum(large, REL_BUCKETS - 1)
    return jnp.where(n < max_exact, n, large)


def sweep_query_blocks(block_fn, n_blocks):
    out = lax.map(block_fn, jnp.arange(n_blocks))
    out = jnp.moveaxis(out, 0, 1)
    return out.reshape((out.shape[0], n_blocks * Q_BLOCK) + out.shape[3:])


def diff_attention(q, k, v, lam, rel_table):
    S = q.shape[1]
    scale = HEAD_DIM ** -0.5
    kf = k.astype(jnp.float32)
    table = rel_table.astype(jnp.float32)
    k_idx = jnp.arange(S)

    def block(i):
        q0 = i * Q_BLOCK
        qb = lax.dynamic_slice_in_dim(q, q0, Q_BLOCK, axis=1).astype(jnp.float32)
        s = jnp.einsum('bqhmd,bkhmd->bhmqk', qb, kf) * scale
        rel = (q0 + jnp.arange(Q_BLOCK))[:, None] - k_idx[None, :]
        bias = jnp.moveaxis(table[t5_bucket(rel)][..., :DA_HEADS], -1, 0)
        s = jnp.where(rel >= 0, s + bias[None, :, None], NEG)
        pm = jax.nn.softmax(s, axis=-1)
        w = pm[:, :, 0] - lam * pm[:, :, 1]
        return jnp.einsum('bhqk,bkhe->bqhe', w.astype(v.dtype), v)

    return sweep_query_blocks(block, S // Q_BLOCK)


def mla_attention(q_nope, q_rope, k_nope, k_rope, v):
    S = q_nope.shape[1]
    scale = (MLA_NOPE + MLA_ROPE) ** -0.5
    kn = k_nope.astype(jnp.float32)
    kr = k_rope.astype(jnp.float32)
    k_idx = jnp.arange(S)

    def block(i):
        q0 = i * Q_BLOCK
        qn = lax.dynamic_slice_in_dim(q_nope, q0, Q_BLOCK, axis=1).astype(jnp.float32)
        qr = lax.dynamic_slice_in_dim(q_rope, q0, Q_BLOCK, axis=1).astype(jnp.float32)
        s = (jnp.einsum('bqhd,bkhd->bhqk', qn, kn)
             + jnp.einsum('bqhr,bkr->bhqk', qr, kr)) * scale
        rel = (q0 + jnp.arange(Q_BLOCK))[:, None] - k_idx[None, :]
        s = jnp.where(rel >= 0, s, NEG)
        pm = jax.nn.softmax(s, axis=-1)
        return jnp.einsum('bhqk,bkhd->bqhd', pm.astype(v.dtype), v)

    return sweep_query_blocks(block, S // Q_BLOCK)


def sliding_window_attention(q, k, v, sinks, rel_table):
    B, S = q.shape[:2]
    nb = S // Q_BLOCK
    scale = HEAD_DIM ** -0.5
    qb = q.reshape(B, nb, Q_BLOCK, SW_KV_HEADS, SW_GROUP, HEAD_DIM).astype(jnp.float32)

    def band(t):
        tp = jnp.pad(t, ((0, 0), (Q_BLOCK, 0), (0, 0), (0, 0)))
        tp = tp.reshape(B, nb + 1, Q_BLOCK, SW_KV_HEADS, HEAD_DIM)
        return jnp.concatenate([tp[:, :-1], tp[:, 1:]], axis=2)

    kb = band(k).astype(jnp.float32)
    vb = band(v)
    s = jnp.einsum('bnqhgd,bnkhd->bnhgqk', qb, kb) * scale
    qi = jnp.arange(Q_BLOCK)
    kj = jnp.arange(2 * Q_BLOCK)
    rel = Q_BLOCK + qi[:, None] - kj[None, :]
    bias = jnp.moveaxis(rel_table.astype(jnp.float32)[t5_bucket(rel)][..., DA_HEADS:], -1, 0)
    bias = bias.reshape(SW_KV_HEADS, SW_GROUP, Q_BLOCK, 2 * Q_BLOCK)
    in_window = (rel >= 0) & (rel < SW_WINDOW)
    key_valid = (jnp.arange(nb)[:, None] * Q_BLOCK + kj[None, :] - Q_BLOCK) >= 0
    mask = in_window[None] & key_valid[:, None, :]
    s = jnp.where(mask[None, :, None, None], s + bias, NEG)
    sink = jnp.broadcast_to(sinks.astype(jnp.float32).reshape(SW_KV_HEADS, SW_GROUP, 1, 1),
                            s.shape[:-1] + (1,))
    pm = jax.nn.softmax(jnp.concatenate([s, sink], axis=-1), axis=-1)[..., :-1]
    o = jnp.einsum('bnhgqk,bnkhd->bnqhgd', pm.astype(v.dtype), vb)
    return o.reshape(B, S, SW_WIDTH)


def setup_inputs(seed: int = 0) -> dict:
    key = jax.random.key(seed)
    ks = jax.random.split(key, 24)
    f32 = jnp.float32

    def nrm(k, shape, scale):
        return jax.random.normal(k, shape, f32) * scale

    def gain(k, shape):
        return 1.0 + 0.05 * jax.random.normal(k, shape, f32)

    start = jax.random.randint(ks[2], (BATCH, 1), 0, 4096, dtype=jnp.int32)
    positions = start + jnp.arange(SEQ, dtype=jnp.int32)[None, :]
    return {
        "x": nrm(ks[0], (BATCH, SEQ, D_MODEL), 1.0),
        "p": nrm(ks[1], (DEPTH, BATCH, SEQ, PLE_DIM), 1.0),
        "positions": positions,
        "rel_bias": nrm(ks[3], (REL_BUCKETS, BIAS_HEADS), 0.3),
        "norm_pre": gain(ks[4], (DEPTH, D_MODEL)),
        "norm_post": gain(ks[5], (DEPTH, D_MODEL)),
        "w_in": nrm(ks[6], (DEPTH, D_MODEL, IN_TOTAL), D_MODEL ** -0.5),
        "da_lambda": nrm(ks[7], (DEPTH, 4, HEAD_DIM), 0.1),
        "da_subln": gain(ks[8], (DEPTH, DA_V)),
        "mla_q_norm": gain(ks[9], (DEPTH, MLA_Q_RANK)),
        "mla_w_qb": nrm(ks[10], (DEPTH, MLA_Q_RANK, MLA_HEADS * (MLA_NOPE + MLA_ROPE)), MLA_Q_RANK ** -0.5),
        "mla_kv_norm": gain(ks[11], (DEPTH, MLA_KV_RANK)),
        "mla_w_kvb": nrm(ks[12], (DEPTH, MLA_KV_RANK, MLA_HEADS * (MLA_NOPE + MLA_V)), MLA_KV_RANK ** -0.5),
        "sw_sinks": nrm(ks[13], (DEPTH, SW_HEADS), 0.5),
        "w_br_a": nrm(ks[14], (DEPTH, DA_WIDTH, D_MODEL), DA_WIDTH ** -0.5),
        "w_br_b": nrm(ks[15], (DEPTH, MLA_WIDTH, D_MODEL), MLA_WIDTH ** -0.5),
        "w_br_c": nrm(ks[16], (DEPTH, SW_WIDTH, D_MODEL), SW_WIDTH ** -0.5),
        "w_out": nrm(ks[17], (DEPTH, D_MODEL, D_MODEL), D_MODEL ** -0.5),
        "w_ple_gate": nrm(ks[18], (DEPTH, D_MODEL, D_MODEL), D_MODEL ** -0.5),
        "w_ple_proj": nrm(ks[19], (DEPTH, PLE_DIM, D_MODEL), PLE_DIM ** -0.5),
    }


def reference(x, p, positions, rel_bias, norm_pre, norm_post, w_in, da_lambda, da_subln,
              mla_q_norm, mla_w_qb, mla_kv_norm, mla_w_kvb, sw_sinks,
              w_br_a, w_br_b, w_br_c, w_out, w_ple_gate, w_ple_proj):
    B, S, _ = x.shape
    split_points = tuple(int(c) for c in np.cumsum(IN_SIZES)[:-1])
    for l in range(DEPTH):
        h = rms_norm(x, norm_pre[l])
        u = h @ w_in[l]
        (a_q, a_k, a_v, a_z, b_q, b_kv, b_kr, b_z,
         c_q, c_k, c_v, c_z, g) = jnp.split(u, split_points, axis=-1)

        lam_init = 0.8 - 0.6 * math.exp(-0.3 * l)
        lf = da_lambda[l].astype(jnp.float32)
        lam = jnp.exp(jnp.sum(lf[0] * lf[1])) - jnp.exp(jnp.sum(lf[2] * lf[3])) + lam_init
        oa = diff_attention(a_q.reshape(B, S, DA_HEADS, 2, HEAD_DIM),
                            a_k.reshape(B, S, DA_HEADS, 2, HEAD_DIM),
                            a_v.reshape(B, S, DA_HEADS, DA_V), lam, rel_bias)
        oa = (rms_norm(oa, da_subln[l]) * (1.0 - lam_init)).reshape(B, S, DA_WIDTH)

        qm = (rms_norm(b_q, mla_q_norm[l]) @ mla_w_qb[l]).reshape(B, S, MLA_HEADS, MLA_NOPE + MLA_ROPE)
        q_nope, q_rope = qm[..., :MLA_NOPE], rope(qm[..., MLA_NOPE:], positions)
        kvm = (rms_norm(b_kv, mla_kv_norm[l]) @ mla_w_kvb[l]).reshape(B, S, MLA_HEADS, MLA_NOPE + MLA_V)
        k_nope, v_m = kvm[..., :MLA_NOPE], kvm[..., MLA_NOPE:]
        k_rope = rope(b_kr, positions)
        ob = mla_attention(q_nope, q_rope, k_nope, k_rope, v_m).reshape(B, S, MLA_WIDTH)

        oc = sliding_window_attention(c_q.reshape(B, S, SW_HEADS, HEAD_DIM),
                                      c_k.reshape(B, S, SW_KV_HEADS, HEAD_DIM),
                                      c_v.reshape(B, S, SW_KV_HEADS, HEAD_DIM),
                                      sw_sinks[l], rel_bias)

        gates = jax.nn.sigmoid(g).reshape(B, S, N_BRANCHES, D_MODEL)
        y = (gates[:, :, 0] * ((oa * jax.nn.silu(a_z)) @ w_br_a[l])
             + gates[:, :, 1] * ((ob * jax.nn.silu(b_z)) @ w_br_b[l])
             + gates[:, :, 2] * ((oc * jax.nn.silu(c_z)) @ w_br_c[l]))
        x = x + rms_norm(y @ w_out[l], norm_post[l])

        x = x + jax.nn.sigmoid(x @ w_ple_gate[l]) * (p[l] @ w_ple_proj[l])
    return x
```

```python
import functools
import math

import numpy as np
import jax
import jax.numpy as jnp
from jax import lax
from jax.experimental import pallas as pl
from jax.experimental.pallas import tpu as pltpu

D_MODEL = 1024
BATCH = 8
SEQ = 2048
DEPTH = 2
PLE_DIM = 256
HEAD_DIM = 64
DA_HEADS = 4
DA_V = 2 * HEAD_DIM
DA_QK = DA_HEADS * 2 * HEAD_DIM
DA_WIDTH = DA_HEADS * DA_V
MLA_HEADS = 8
MLA_NOPE = 64
MLA_ROPE = 32
MLA_V = 64
MLA_Q_RANK = 256
MLA_KV_RANK = 128
MLA_WIDTH = MLA_HEADS * MLA_V
SW_HEADS = 8
SW_KV_HEADS = 2
SW_WINDOW = 128
SW_WIDTH = SW_HEADS * HEAD_DIM
REL_BUCKETS = 32
REL_MAX_DIST = 128
BIAS_HEADS = DA_HEADS + SW_HEADS
ROPE_THETA = 10000.0
EPS = 1e-6
NEG = -1e30

IN_SIZES = (DA_QK, DA_QK, DA_WIDTH, DA_WIDTH,
            MLA_Q_RANK, MLA_KV_RANK, MLA_ROPE, MLA_WIDTH,
            SW_HEADS * HEAD_DIM, SW_KV_HEADS * HEAD_DIM, SW_KV_HEADS * HEAD_DIM, SW_WIDTH,
            3 * D_MODEL)
_OFF = tuple(int(c) for c in np.cumsum((0,) + IN_SIZES))

LANES = 128
TOKENS = BATCH * SEQ
ROW_TILE = 512
ATT_TILE = 512
ATT_NQ = SEQ // ATT_TILE
SW_TILE = 256
SW_NQ = SEQ // SW_TILE
SW_KEYS = SW_TILE + SW_WINDOW
DA_SCALE = HEAD_DIM ** -0.5
SW_SCALE = HEAD_DIM ** -0.5
MLA_SCALE = (MLA_NOPE + MLA_ROPE) ** -0.5
BF16 = jnp.bfloat16
F32 = jnp.float32

_W1_SIZES = (512, 512, 512, 512, 128, 128, 256, 128, 128)
_W1_OFF = tuple(int(c) for c in np.cumsum((0,) + _W1_SIZES))


def _bucket_thresholds():
    max_exact = REL_BUCKETS // 2
    n = np.arange(0, REL_MAX_DIST + 1)
    nf = np.maximum(n, 1).astype(np.float64)
    large = max_exact + (np.log(nf / max_exact) / math.log(REL_MAX_DIST / max_exact)
                         * (REL_BUCKETS - max_exact)).astype(np.int64)
    bucket = np.where(n < max_exact, n, np.minimum(large, REL_BUCKETS - 1))
    return tuple(int(np.argmax(bucket >= j)) for j in range(REL_BUCKETS))


_THRESH = _bucket_thresholds()


def _dot(a, b):
    return jnp.dot(a, b, preferred_element_type=F32)


def _dot_nt(a, b):
    return lax.dot_general(a, b, (((1,), (1,)), ((), ())), preferred_element_type=F32)


def _rms(x, w):
    y = x * lax.rsqrt(jnp.mean(x * x, axis=-1, keepdims=True) + EPS)
    return y * w


def _sigmoid(x):
    return 1.0 / (1.0 + jnp.exp(-x))


def _fold(x, op):
    r = x[:, :LANES]
    for c in range(1, x.shape[1] // LANES):
        r = op(r, x[:, c * LANES:(c + 1) * LANES])
    return r


def _const_spec(shape):
    nd = len(shape)
    return pl.BlockSpec(shape, lambda *_: (0,) * nd, pipeline_mode=pl.Buffered(1))


def _bias_of(rel, tab_ref, head):
    b = jnp.full(rel.shape, tab_ref[head], F32)
    for j in range(1, REL_BUCKETS):
        b = jnp.where(rel >= _THRESH[j], tab_ref[j * BIAS_HEADS + head], b)
    return b


def _da_bias_kernel(tab_ref, out_ref):
    h = pl.program_id(0)
    i = lax.broadcasted_iota(jnp.int32, (ATT_TILE, ATT_TILE), 0)
    j = lax.broadcasted_iota(jnp.int32, (ATT_TILE, ATT_TILE), 1)
    rel = i - j
    out_ref[0, 0] = jnp.where(rel >= 0, _bias_of(rel, tab_ref, h), NEG)
    out_ref[0, 1] = _bias_of(rel + ATT_TILE, tab_ref, h)


def _sw_bias_kernel(tab_ref, out_ref):
    t = pl.program_id(0)
    h = pl.program_id(1)
    i = lax.broadcasted_iota(jnp.int32, (SW_TILE, SW_KEYS), 0)
    j = lax.broadcasted_iota(jnp.int32, (SW_TILE, SW_KEYS), 1)
    rel = i - j + t * SW_WINDOW
    ok = (rel >= 0) & (rel < SW_WINDOW)
    out_ref[0, 0] = jnp.where(ok, _bias_of(rel, tab_ref, DA_HEADS + h), NEG)


def _bias_tiles(rel_bias):
    tab = rel_bias.astype(F32).reshape(REL_BUCKETS * BIAS_HEADS)
    smem = pl.BlockSpec(memory_space=pltpu.SMEM)
    da = pl.pallas_call(
        _da_bias_kernel,
        grid=(DA_HEADS,),
        in_specs=[smem],
        out_specs=pl.BlockSpec((1, 2, ATT_TILE, ATT_TILE), lambda h: (h, 0, 0, 0)),
        out_shape=jax.ShapeDtypeStruct((DA_HEADS, 2, ATT_TILE, ATT_TILE), F32),
        name="da_bias_tiles",
    )(tab)
    sw = pl.pallas_call(
        _sw_bias_kernel,
        grid=(2, SW_HEADS),
        in_specs=[smem],
        out_specs=pl.BlockSpec((1, 1, SW_TILE, SW_KEYS), lambda t, h: (t, h, 0, 0)),
        out_shape=jax.ShapeDtypeStruct((2, SW_HEADS, SW_TILE, SW_KEYS), F32),
        name="sw_bias_tiles",
    )(tab)
    return tab, da, sw


def _inproj_kernel(x_ref, pos_ref, npre_ref, w1_ref, qn_ref, wqb_ref, kvn_ref, wk_ref, wv_ref,
                   invf_ref, aq_ref, ak_ref, av_ref, cq_ref, ck_ref, cv_ref, mq_ref, mk_ref,
                   mv_ref):
    h = _rms(x_ref[...], npre_ref[...]).astype(BF16)

    def proj(i):
        return _dot(h, w1_ref[:, _W1_OFF[i]:_W1_OFF[i + 1]])

    aq_ref[...] = (proj(0) * DA_SCALE).astype(BF16)
    ak_ref[...] = proj(1).astype(BF16)
    av_ref[...] = proj(2).astype(BF16)
    cq_ref[...] = (proj(3) * SW_SCALE).astype(BF16)
    ck_ref[...] = proj(4).astype(BF16)
    cv_ref[...] = proj(5).astype(BF16)
    bq = proj(6)
    bkv = proj(7)
    kr = proj(8)

    ang = pos_ref[...].astype(F32) * invf_ref[...]
    cos = jnp.cos(ang)
    sin = jnp.sin(ang)
    lane = lax.broadcasted_iota(jnp.int32, (1, LANES), 1)
    first = (lane >= MLA_NOPE) & (lane < MLA_NOPE + MLA_ROPE // 2)
    second = (lane >= MLA_NOPE + MLA_ROPE // 2) & (lane < MLA_NOPE + MLA_ROPE)
    sin_a = jnp.where(first, -sin, 0.0)
    sin_b = jnp.where(second, sin, 0.0)

    def rope(t):
        return (t * cos + pltpu.roll(t, LANES - MLA_ROPE // 2, 1) * sin_a
                + pltpu.roll(t, MLA_ROPE // 2, 1) * sin_b)

    qm = _dot(_rms(bq, qn_ref[...]).astype(BF16), wqb_ref[...])
    kvn = _rms(bkv, kvn_ref[...]).astype(BF16)
    km = _dot(kvn, wk_ref[...])
    mv_ref[...] = _dot(kvn, wv_ref[...]).astype(BF16)
    kr = rope(kr)
    for hh in range(MLA_HEADS):
        sl = slice(hh * LANES, (hh + 1) * LANES)
        mq_ref[:, sl] = (rope(qm[:, sl]) * MLA_SCALE).astype(BF16)
        mk_ref[:, sl] = (km[:, sl] + kr).astype(BF16)


def _inproj(x2, pos2, npre, w1, qn, wqb, kvn, wk, wv, invf):
    nt = TOKENS // ROW_TILE
    row = lambda w: pl.BlockSpec((ROW_TILE, w), lambda i: (i, 0))
    widths = (512, 512, 512, 512, 128, 128, 1024, 1024, 512)
    return pl.pallas_call(
        _inproj_kernel,
        grid=(nt,),
        in_specs=[row(D_MODEL), row(1), _const_spec(npre.shape), _const_spec(w1.shape),
                  _const_spec(qn.shape), _const_spec(wqb.shape), _const_spec(kvn.shape),
                  _const_spec(wk.shape), _const_spec(wv.shape), _const_spec(invf.shape)],
        out_specs=[row(w) for w in widths],
        out_shape=[jax.ShapeDtypeStruct((TOKENS, w), BF16) for w in widths],
        compiler_params=pltpu.CompilerParams(dimension_semantics=("parallel",)),
        name="inproj",
    )(x2, pos2, npre, w1, qn, wqb, kvn, wk, wv, invf)


def _da_kernel(tab_ref, lam_ref, subln_ref, q_ref, k_ref, v_ref, bias_ref, o_ref,
               s_ref, m_ref, l_ref, acc_ref, *, lam_init):
    h = pl.program_id(1)
    qi = pl.program_id(2)
    q = q_ref[...]
    lane = lax.broadcasted_iota(jnp.int32, (1, LANES), 1)
    lo = lane < HEAD_DIM
    far_bias = tab_ref[(REL_BUCKETS - 1) * BIAS_HEADS + h]

    def scores(row0):
        k = k_ref[pl.ds(row0, ATT_TILE), :]
        zero = jnp.zeros_like(k)
        return _dot_nt(q, jnp.where(lo, k, zero)), _dot_nt(q, jnp.where(lo, zero, k))

    def put(slot, s, first=False):
        for m in range(2):
            s_ref[m, slot] = s[m]
            f = _fold(s[m], jnp.maximum)
            m_ref[m] = f if first else jnp.maximum(m_ref[m], f)

    row_diag = pl.multiple_of(qi * ATT_TILE, ATT_TILE)
    row_prev = pl.multiple_of(jnp.maximum(qi - 1, 0) * ATT_TILE, ATT_TILE)
    s1, s2 = scores(row_diag)
    put(0, (s1 + bias_ref[0, 0], s2 + bias_ref[0, 0]), first=True)
    prev_bias = bias_ref[0, 1] + jnp.where(qi == 0, NEG, 0.0)
    s1, s2 = scores(row_prev)
    put(1, (s1 + prev_bias, s2 + prev_bias))
    nfar = jnp.maximum(qi - 1, 0)

    def far_scores(j, c):
        t1, t2 = scores(pl.multiple_of(j * ATT_TILE, ATT_TILE))
        put(2 + j, (t1 + far_bias, t2 + far_bias))
        return c

    lax.fori_loop(0, nfar, far_scores, 0)

    mx = [jnp.max(m_ref[m], axis=-1, keepdims=True) for m in range(2)]

    def accum(slot, row0, first=False):
        v = v_ref[pl.ds(row0, ATT_TILE), :]
        for m in range(2):
            e = jnp.exp(s_ref[m, slot] - mx[m])
            fl = _fold(e, jnp.add)
            pv = _dot(e.astype(BF16), v)
            if first:
                l_ref[m] = fl
                acc_ref[m] = pv
            else:
                l_ref[m] += fl
                acc_ref[m] += pv

    accum(0, row_diag, first=True)
    accum(1, row_prev)

    def far_accum(j, c):
        accum(2 + j, pl.multiple_of(j * ATT_TILE, ATT_TILE))
        return c

    lax.fori_loop(0, nfar, far_accum, 0)

    o1 = acc_ref[0] / jnp.sum(l_ref[0], axis=-1, keepdims=True)
    o2 = acc_ref[1] / jnp.sum(l_ref[1], axis=-1, keepdims=True)
    lf = lam_ref[...]
    lam = (jnp.exp(jnp.sum(lf[0:1] * lf[1:2], axis=-1, keepdims=True))
           - jnp.exp(jnp.sum(lf[2:3] * lf[3:4], axis=-1, keepdims=True)) + lam_init)
    o = o1 - lam * o2
    o_ref[...] = _rms(o, subln_ref[...]) * (1.0 - lam_init)


def _da_attention(tab, lam_p, subln, aq, ak, av, da_bias, lam_init):
    smem = pl.BlockSpec(memory_space=pltpu.SMEM)
    return pl.pallas_call(
        functools.partial(_da_kernel, lam_init=lam_init),
        grid=(BATCH, DA_HEADS, ATT_NQ),
        in_specs=[smem, _const_spec(lam_p.shape), _const_spec(subln.shape),
                  pl.BlockSpec((ATT_TILE, LANES), lambda b, h, i: (b * ATT_NQ + i, h)),
                  pl.BlockSpec((SEQ, LANES), lambda b, h, i: (b, h)),
                  pl.BlockSpec((SEQ, LANES), lambda b, h, i: (b, h)),
                  pl.BlockSpec((1, 2, ATT_TILE, ATT_TILE), lambda b, h, i: (h, 0, 0, 0))],
        out_specs=pl.BlockSpec((ATT_TILE, LANES), lambda b, h, i: (b * ATT_NQ + i, h)),
        out_shape=jax.ShapeDtypeStruct((TOKENS, DA_WIDTH), F32),
        scratch_shapes=[pltpu.VMEM((2, ATT_NQ, ATT_TILE, ATT_TILE), F32),
                        pltpu.VMEM((2, ATT_TILE, LANES), F32),
                        pltpu.VMEM((2, ATT_TILE, LANES), F32),
                        pltpu.VMEM((2, ATT_TILE, LANES), F32)],
        compiler_params=pltpu.CompilerParams(
            dimension_semantics=("parallel", "parallel", "parallel")),
        name="diff_attention",
    )(tab, lam_p, subln, aq, ak, av, da_bias)


def _mla_kernel(q_ref, k_ref, v_ref, o_ref, s_ref, m_ref, l_ref, acc_ref):
    qi = pl.program_id(2)
    q = [q_ref[:, a * LANES:(a + 1) * LANES] for a in range(2)]

    def scores(row0):
        return [_dot_nt(q[a], k_ref[pl.ds(row0, ATT_TILE), a * LANES:(a + 1) * LANES])
                for a in range(2)]

    def put(slot, s, first=False):
        for a in range(2):
            s_ref[a, slot] = s[a]
            f = _fold(s[a], jnp.maximum)
            m_ref[a] = f if first else jnp.maximum(m_ref[a], f)

    row_diag = pl.multiple_of(qi * ATT_TILE, ATT_TILE)
    i = lax.broadcasted_iota(jnp.int32, (ATT_TILE, ATT_TILE), 0)
    j = lax.broadcasted_iota(jnp.int32, (ATT_TILE, ATT_TILE), 1)
    causal = i >= j
    put(0, [jnp.where(causal, s, NEG) for s in scores(row_diag)], first=True)

    def far_scores(c, carry):
        put(1 + c, scores(pl.multiple_of(c * ATT_TILE, ATT_TILE)))
        return carry

    lax.fori_loop(0, qi, far_scores, 0)

    mx = [jnp.max(m_ref[a], axis=-1, keepdims=True) for a in range(2)]

    def accum(slot, row0, first=False):
        v = v_ref[pl.ds(row0, ATT_TILE), :]
        for a in range(2):
            e = jnp.exp(s_ref[a, slot] - mx[a])
            fl = _fold(e, jnp.add)
            pv = _dot(e.astype(BF16), v)
            if first:
                l_ref[a] = fl
                acc_ref[a] = pv
            else:
                l_ref[a] += fl
                acc_ref[a] += pv

    accum(0, row_diag, first=True)

    def far_accum(c, carry):
        accum(1 + c, pl.multiple_of(c * ATT_TILE, ATT_TILE))
        return carry

    lax.fori_loop(0, qi, far_accum, 0)

    o = [acc_ref[a] / jnp.sum(l_ref[a], axis=-1, keepdims=True) for a in range(2)]
    lane = lax.broadcasted_iota(jnp.int32, (1, LANES), 1)
    o_ref[...] = jnp.where(lane < MLA_V, o[0], o[1])


def _mla_attention(mq, mk, mv):
    return pl.pallas_call(
        _mla_kernel,
        grid=(BATCH, MLA_HEADS // 2, ATT_NQ),
        in_specs=[pl.BlockSpec((ATT_TILE, 2 * LANES), lambda b, h, i: (b * ATT_NQ + i, h)),
                  pl.BlockSpec((SEQ, 2 * LANES), lambda b, h, i: (b, h)),
                  pl.BlockSpec((SEQ, LANES), lambda b, h, i: (b, h))],
        out_specs=pl.BlockSpec((ATT_TILE, LANES), lambda b, h, i: (b * ATT_NQ + i, h)),
        out_shape=jax.ShapeDtypeStruct((TOKENS, MLA_WIDTH), F32),
        scratch_shapes=[pltpu.VMEM((2, ATT_NQ, ATT_TILE, ATT_TILE), F32),
                        pltpu.VMEM((2, ATT_TILE, LANES), F32),
                        pltpu.VMEM((2, ATT_TILE, LANES), F32),
                        pltpu.VMEM((2, ATT_TILE, LANES), F32)],
        compiler_params=pltpu.CompilerParams(
            dimension_semantics=("parallel", "parallel", "parallel")),
        name="mla_attention",
    )(mq, mk, mv)


def _sw_kernel(sink_ref, q_ref, k_ref, v_ref, bias_ref, o_ref):
    qi = pl.program_id(1)
    row0 = pl.multiple_of(jnp.maximum(qi * SW_TILE - SW_WINDOW, 0), SW_WINDOW)
    k = k_ref[pl.ds(row0, SW_KEYS), :]
    v = v_ref[pl.ds(row0, SW_KEYS), :]
    lane = lax.broadcasted_iota(jnp.int32, (1, LANES), 1)
    lo = lane < HEAD_DIM
    zero = jnp.zeros_like(k)
    k_lo = [jnp.where(lo, k, zero), None]
    k_hi = [None, jnp.where(lo, zero, k)]
    k_hi[0] = pltpu.roll(k_lo[0], HEAD_DIM, 1)
    k_lo[1] = pltpu.roll(k_hi[1], HEAD_DIM, 1)
    v_swap = pltpu.roll(v, HEAD_DIM, 1)

    for hp in range(SW_HEADS // 2):
        kvh = hp // 2
        q = q_ref[:, hp * LANES:(hp + 1) * LANES]
        outs = []
        for a, kk in enumerate((k_lo[kvh], k_hi[kvh])):
            head = 2 * hp + a
            s = _dot_nt(q, kk) + bias_ref[0, head]
            sink = sink_ref[head]
            m = jnp.maximum(jnp.max(s, axis=-1, keepdims=True), sink)
            e = jnp.exp(s - m)
            l = jnp.sum(e, axis=-1, keepdims=True) + jnp.exp(sink - m)
            vv = v if (a == kvh) else v_swap
            outs.append(_dot(e.astype(BF16), vv) / l)
        o_ref[:, hp * LANES:(hp + 1) * LANES] = jnp.where(lo, outs[0], outs[1])


def _sw_attention(sinks, cq, ck, cv, sw_bias):
    smem = pl.BlockSpec(memory_space=pltpu.SMEM)
    return pl.pallas_call(
        _sw_kernel,
        grid=(BATCH, SW_NQ),
        in_specs=[smem,
                  pl.BlockSpec((SW_TILE, SW_WIDTH), lambda b, i: (b * SW_NQ + i, 0)),
                  pl.BlockSpec((SEQ, LANES), lambda b, i: (b, 0)),
                  pl.BlockSpec((SEQ, LANES), lambda b, i: (b, 0)),
                  pl.BlockSpec((1, SW_HEADS, SW_TILE, SW_KEYS),
                               lambda b, i: (jnp.minimum(i, 1), 0, 0, 0))],
        out_specs=pl.BlockSpec((SW_TILE, SW_WIDTH), lambda b, i: (b * SW_NQ + i, 0)),
        out_shape=jax.ShapeDtypeStruct((TOKENS, SW_WIDTH), F32),
        compiler_params=pltpu.CompilerParams(dimension_semantics=("parallel", "parallel")),
        name="sw_attention",
    )(sinks, cq, ck, cv, sw_bias)


def _merge_kernel(x_ref, p_ref, oa_ref, ob_ref, oc_ref, npre_ref, npost_ref, wz_ref, wg_ref,
                  wbr_ref, wout_ref, wpg_ref, wpp_ref, out_ref):
    x = x_ref[...]
    h = _rms(x, npre_ref[...]).astype(BF16)
    y = None
    for i, o_ref in enumerate((oa_ref, ob_ref, oc_ref)):
        z = _dot(h, wz_ref[:, i * 512:(i + 1) * 512])
        t = (o_ref[...] * (z * _sigmoid(z))).astype(BF16)
        br = _dot(t, wbr_ref[i])
        g = _sigmoid(_dot(h, wg_ref[:, i * D_MODEL:(i + 1) * D_MODEL]))
        y = g * br if y is None else y + g * br
    x1 = x + _rms(_dot(y.astype(BF16), wout_ref[...]), npost_ref[...])
    u = _dot(x1.astype(BF16), wpg_ref[...])
    pp = _dot(p_ref[...].astype(BF16), wpp_ref[...])
    out_ref[...] = x1 + _sigmoid(u) * pp


def _merge(x2, p2, oa, ob, oc, npre, npost, wz, wg, wbr, wout, wpg, wpp):
    nt = TOKENS // ROW_TILE
    row = lambda w: pl.BlockSpec((ROW_TILE, w), lambda i: (i, 0))
    return pl.pallas_call(
        _merge_kernel,
        grid=(nt,),
        in_specs=[row(D_MODEL), row(PLE_DIM), row(512), row(512), row(512),
                  _const_spec(npre.shape), _const_spec(npost.shape), _const_spec(wz.shape),
                  _const_spec(wg.shape), _const_spec(wbr.shape), _const_spec(wout.shape),
                  _const_spec(wpg.shape), _const_spec(wpp.shape)],
        out_specs=row(D_MODEL),
        out_shape=jax.ShapeDtypeStruct((TOKENS, D_MODEL), F32),
        compiler_params=pltpu.CompilerParams(dimension_semantics=("parallel",)),
        name="gated_merge",
    )(x2, p2, oa, ob, oc, npre, npost, wz, wg, wbr, wout, wpg, wpp)


def _layer_weights(w_in_l, mla_w_qb_l, mla_w_kvb_l):
    col = lambda i: w_in_l[:, _OFF[i]:_OFF[i + 1]]
    zeros = lambda n: jnp.zeros((D_MODEL, n), w_in_l.dtype)
    kr_slab = jnp.concatenate([zeros(MLA_NOPE), col(6), zeros(LANES - MLA_NOPE - MLA_ROPE)], 1)
    w1 = jnp.concatenate([col(0), col(1), col(2), col(8), col(9), col(10), col(4), col(5),
                          kr_slab], axis=1).astype(BF16)
    wz = jnp.concatenate([col(3), col(7), col(11)], axis=1).astype(BF16)
    wg = col(12).astype(BF16)
    wqb = mla_w_qb_l.reshape(MLA_Q_RANK, MLA_HEADS, MLA_NOPE + MLA_ROPE)
    wqb = jnp.pad(wqb, ((0, 0), (0, 0), (0, LANES - MLA_NOPE - MLA_ROPE)))
    wqb = wqb.reshape(MLA_Q_RANK, MLA_HEADS * LANES).astype(BF16)
    wkv = mla_w_kvb_l.reshape(MLA_KV_RANK, MLA_HEADS, MLA_NOPE + MLA_V)
    wk = jnp.pad(wkv[:, :, :MLA_NOPE], ((0, 0), (0, 0), (0, LANES - MLA_NOPE)))
    wk = wk.reshape(MLA_KV_RANK, MLA_HEADS * LANES).astype(BF16)
    wv = wkv[:, :, MLA_NOPE:].reshape(MLA_KV_RANK, MLA_WIDTH).astype(BF16)
    return w1, wz, wg, wqb, wk, wv


def kernel(x, p, positions, rel_bias, norm_pre, norm_post, w_in, da_lambda, da_subln, mla_q_norm,
           mla_w_qb, mla_kv_norm, mla_w_kvb, sw_sinks, w_br_a, w_br_b, w_br_c, w_out, w_ple_gate,
           w_ple_proj):
    assert x.shape == (BATCH, SEQ, D_MODEL) and p.shape == (DEPTH, BATCH, SEQ, PLE_DIM)
    x2 = x.reshape(TOKENS, D_MODEL).astype(F32)
    pos2 = positions.reshape(TOKENS, 1).astype(jnp.int32)
    half = MLA_ROPE // 2
    inv_freq = ROPE_THETA ** (-jnp.arange(half, dtype=F32) / half)
    invf = jnp.zeros((1, LANES), F32)
    invf = invf.at[0, MLA_NOPE:MLA_NOPE + half].set(inv_freq)
    invf = invf.at[0, MLA_NOPE + half:MLA_NOPE + 2 * half].set(inv_freq)
    tab, da_bias, sw_bias = _bias_tiles(rel_bias)

    for l in range(DEPTH):
        w1, wz, wg, wqb, wk, wv = _layer_weights(w_in[l], mla_w_qb[l], mla_w_kvb[l])
        npre = norm_pre[l].reshape(1, D_MODEL).astype(F32)
        npost = norm_post[l].reshape(1, D_MODEL).astype(F32)
        aq, ak, av, cq, ck, cv, mq, mk, mv = _inproj(
            x2, pos2, npre, w1, mla_q_norm[l].reshape(1, MLA_Q_RANK).astype(F32), wqb,
            mla_kv_norm[l].reshape(1, MLA_KV_RANK).astype(F32), wk, wv, invf)
        lam_init = 0.8 - 0.6 * math.exp(-0.3 * l)
        oa = _da_attention(tab, da_lambda[l].astype(F32), da_subln[l].reshape(1, DA_V).astype(F32),
                           aq, ak, av, da_bias, lam_init)
        ob = _mla_attention(mq, mk, mv)
        oc = _sw_attention(sw_sinks[l].astype(F32), cq, ck, cv, sw_bias)
        wbr = jnp.stack([w_br_a[l], w_br_b[l], w_br_c[l]]).astype(BF16)
        x2 = _merge(x2, p[l].reshape(TOKENS, PLE_DIM).astype(F32), oa, ob, oc, npre, npost, wz, wg,
                    wbr, w_out[l].astype(BF16), w_ple_gate[l].astype(BF16),
                    w_ple_proj[l].astype(BF16))
    return x2.reshape(BATCH, SEQ, D_MODEL)
```

```python
import functools
import math

import numpy as np
import jax
import jax.numpy as jnp
from jax import lax
from jax.experimental import pallas as pl
from jax.experimental.pallas import tpu as pltpu

D_MODEL = 1024
BATCH = 8
SEQ = 2048
DEPTH = 2
PLE_DIM = 256
HEAD_DIM = 64
DA_HEADS = 4
DA_V = 2 * HEAD_DIM
DA_QK = DA_HEADS * 2 * HEAD_DIM
DA_WIDTH = DA_HEADS * DA_V
MLA_HEADS = 8
MLA_NOPE = 64
MLA_ROPE = 32
MLA_V = 64
MLA_Q_RANK = 256
MLA_KV_RANK = 128
MLA_WIDTH = MLA_HEADS * MLA_V
SW_HEADS = 8
SW_KV_HEADS = 2
SW_WINDOW = 128
SW_WIDTH = SW_HEADS * HEAD_DIM
REL_BUCKETS = 32
REL_MAX_DIST = 128
BIAS_HEADS = DA_HEADS + SW_HEADS
ROPE_THETA = 10000.0
EPS = 1e-6
NEG = -1e30

IN_SIZES = (DA_QK, DA_QK, DA_WIDTH, DA_WIDTH,
            MLA_Q_RANK, MLA_KV_RANK, MLA_ROPE, MLA_WIDTH,
            SW_HEADS * HEAD_DIM, SW_KV_HEADS * HEAD_DIM, SW_KV_HEADS * HEAD_DIM, SW_WIDTH,
            3 * D_MODEL)
_OFF = tuple(int(c) for c in np.cumsum((0,) + IN_SIZES))

LANES = 128
TOKENS = BATCH * SEQ
ROW_TILE = 512
ATT_TILE = 512
ATT_NQ = SEQ // ATT_TILE
SW_TILE = 256
SW_NQ = SEQ // SW_TILE
SW_KEYS = SW_TILE + SW_WINDOW
LOG2E = math.log2(math.e)
DA_SCALE = HEAD_DIM ** -0.5 * LOG2E
SW_SCALE = HEAD_DIM ** -0.5
MLA_SCALE = (MLA_NOPE + MLA_ROPE) ** -0.5 * LOG2E
BF16 = jnp.bfloat16
F32 = jnp.float32

_W1_SIZES = (512, 512, 512, 512, 128, 128, 256, 128, 128)
_W1_OFF = tuple(int(c) for c in np.cumsum((0,) + _W1_SIZES))


def _bucket_thresholds():
    max_exact = REL_BUCKETS // 2
    n = np.arange(0, REL_MAX_DIST + 1)
    nf = np.maximum(n, 1).astype(np.float64)
    large = max_exact + (np.log(nf / max_exact) / math.log(REL_MAX_DIST / max_exact)
                         * (REL_BUCKETS - max_exact)).astype(np.int64)
    bucket = np.where(n < max_exact, n, np.minimum(large, REL_BUCKETS - 1))
    return tuple(int(np.argmax(bucket >= j)) for j in range(REL_BUCKETS))


_THRESH = _bucket_thresholds()


def _dot(a, b):
    return jnp.dot(a, b, preferred_element_type=F32)


def _dot_nt(a, b):
    return lax.dot_general(a, b, (((1,), (1,)), ((), ())), preferred_element_type=F32)


def _rms(x, w):
    y = x * lax.rsqrt(jnp.mean(x * x, axis=-1, keepdims=True) + EPS)
    return y * w


def _sigmoid(x):
    return 1.0 / (1.0 + jnp.exp(-x))


def _const_spec(shape):
    nd = len(shape)
    return pl.BlockSpec(shape, lambda *_: (0,) * nd, pipeline_mode=pl.Buffered(1))


def _bias_of(rel, tab_ref, head):
    b = jnp.full(rel.shape, tab_ref[head], F32)
    for j in range(1, REL_BUCKETS):
        b = jnp.where(rel >= _THRESH[j], tab_ref[j * BIAS_HEADS + head], b)
    return b


def _da_bias_kernel(tab_ref, out_ref):
    h = pl.program_id(0)
    i = lax.broadcasted_iota(jnp.int32, (ATT_TILE, ATT_TILE), 0)
    j = lax.broadcasted_iota(jnp.int32, (ATT_TILE, ATT_TILE), 1)
    rel = i - j
    out_ref[0, 0] = jnp.where(rel >= 0, _bias_of(rel, tab_ref, h) * LOG2E, NEG)
    out_ref[0, 1] = _bias_of(rel + ATT_TILE, tab_ref, h) * LOG2E


def _sw_bias_kernel(tab_ref, out_ref):
    t = pl.program_id(0)
    h = pl.program_id(1)
    i = lax.broadcasted_iota(jnp.int32, (SW_TILE, SW_KEYS), 0)
    j = lax.broadcasted_iota(jnp.int32, (SW_TILE, SW_KEYS), 1)
    rel = i - j + t * SW_WINDOW
    ok = (rel >= 0) & (rel < SW_WINDOW)
    out_ref[0, 0] = jnp.where(ok, _bias_of(rel, tab_ref, DA_HEADS + h), NEG)


def _bias_tiles(rel_bias):
    tab = rel_bias.astype(F32).reshape(REL_BUCKETS * BIAS_HEADS)
    smem = pl.BlockSpec(memory_space=pltpu.SMEM)
    da = pl.pallas_call(
        _da_bias_kernel,
        grid=(DA_HEADS,),
        in_specs=[smem],
        out_specs=pl.BlockSpec((1, 2, ATT_TILE, ATT_TILE), lambda h: (h, 0, 0, 0)),
        out_shape=jax.ShapeDtypeStruct((DA_HEADS, 2, ATT_TILE, ATT_TILE), F32),
        name="da_bias_tiles",
    )(tab)
    sw = pl.pallas_call(
        _sw_bias_kernel,
        grid=(2, SW_HEADS),
        in_specs=[smem],
        out_specs=pl.BlockSpec((1, 1, SW_TILE, SW_KEYS), lambda t, h: (t, h, 0, 0)),
        out_shape=jax.ShapeDtypeStruct((2, SW_HEADS, SW_TILE, SW_KEYS), F32),
        name="sw_bias_tiles",
    )(tab)
    return tab, da, sw


def _inproj_kernel(x_ref, pos_ref, npre_ref, w1_ref, qn_ref, wqb_ref, kvn_ref, wk_ref, wv_ref,
                   invf_ref, aq_ref, ak_ref, av_ref, cq_ref, ck_ref, cv_ref, mq_ref, mk_ref,
                   mv_ref):
    h = _rms(x_ref[...], npre_ref[...]).astype(BF16)

    def proj(i):
        return _dot(h, w1_ref[:, _W1_OFF[i]:_W1_OFF[i + 1]])

    aq_ref[...] = (proj(0) * DA_SCALE).astype(BF16)
    ak_ref[...] = proj(1).astype(BF16)
    av_ref[...] = proj(2).astype(BF16)
    cq_ref[...] = (proj(3) * SW_SCALE).astype(BF16)
    ck_ref[...] = proj(4).astype(BF16)
    cv_ref[...] = proj(5).astype(BF16)
    bq = proj(6)
    bkv = proj(7)
    kr = proj(8)

    ang = pos_ref[...].astype(F32) * invf_ref[...]
    cos = jnp.cos(ang)
    sin = jnp.sin(ang)
    lane = lax.broadcasted_iota(jnp.int32, (1, LANES), 1)
    first = (lane >= MLA_NOPE) & (lane < MLA_NOPE + MLA_ROPE // 2)
    second = (lane >= MLA_NOPE + MLA_ROPE // 2) & (lane < MLA_NOPE + MLA_ROPE)
    sin_a = jnp.where(first, -sin, 0.0)
    sin_b = jnp.where(second, sin, 0.0)

    def rope(t):
        return (t * cos + pltpu.roll(t, LANES - MLA_ROPE // 2, 1) * sin_a
                + pltpu.roll(t, MLA_ROPE // 2, 1) * sin_b)

    qm = _dot(_rms(bq, qn_ref[...]).astype(BF16), wqb_ref[...])
    kvn = _rms(bkv, kvn_ref[...]).astype(BF16)
    km = _dot(kvn, wk_ref[...])
    mv_ref[...] = _dot(kvn, wv_ref[...]).astype(BF16)
    kr = rope(kr)
    for hh in range(MLA_HEADS):
        sl = slice(hh * LANES, (hh + 1) * LANES)
        mq_ref[:, sl] = (rope(qm[:, sl]) * MLA_SCALE).astype(BF16)
        mk_ref[:, sl] = (km[:, sl] + kr).astype(BF16)


def _inproj(x2, pos2, npre, w1, qn, wqb, kvn, wk, wv, invf):
    nt = TOKENS // ROW_TILE
    row = lambda w: pl.BlockSpec((ROW_TILE, w), lambda i: (i, 0))
    widths = (512, 512, 512, 512, 128, 128, 1024, 1024, 512)
    return pl.pallas_call(
        _inproj_kernel,
        grid=(nt,),
        in_specs=[row(D_MODEL), row(1), _const_spec(npre.shape), _const_spec(w1.shape),
                  _const_spec(qn.shape), _const_spec(wqb.shape), _const_spec(kvn.shape),
                  _const_spec(wk.shape), _const_spec(wv.shape), _const_spec(invf.shape)],
        out_specs=[row(w) for w in widths],
        out_shape=[jax.ShapeDtypeStruct((TOKENS, w), BF16) for w in widths],
        compiler_params=pltpu.CompilerParams(dimension_semantics=("parallel",)),
        name="inproj",
    )(x2, pos2, npre, w1, qn, wqb, kvn, wk, wv, invf)


def _with_ones(v):
    return jnp.concatenate([v, jnp.ones(v.shape, v.dtype)], axis=1)


def _per_q_tile(body):
    qi = pl.program_id(2)
    for n in range(ATT_NQ):
        pl.when(qi == n)(functools.partial(body, n))


def _da_kernel(tab_ref, lam_ref, subln_ref, q_ref, k_ref, v_ref, bias_ref, o_ref, *, lam_init):
    t = ATT_TILE
    h = pl.program_id(1)
    far_bias = tab_ref[(REL_BUCKETS - 1) * BIAS_HEADS + h] * LOG2E
    lane = lax.broadcasted_iota(jnp.int32, (1, LANES), 1)
    lo = lane < HEAD_DIM

    def body(n):
        q = q_ref[...]
        zero = jnp.zeros_like(q)
        p = []
        for qh in (jnp.where(lo, q, zero), jnp.where(lo, zero, q)):
            s_d = _dot_nt(qh, k_ref[n * t:(n + 1) * t, :]) + bias_ref[0, 0]
            mx = jnp.max(s_d, axis=-1, keepdims=True)
            if n >= 1:
                s_p = _dot_nt(qh, k_ref[(n - 1) * t:n * t, :]) + bias_ref[0, 1]
                mx = jnp.maximum(mx, jnp.max(s_p, axis=-1, keepdims=True))
            if n >= 2:
                s_f = _dot_nt(qh, k_ref[0:(n - 1) * t, :])
                mx = jnp.maximum(mx, jnp.max(s_f, axis=-1, keepdims=True) + far_bias)
            parts = []
            if n >= 2:
                parts.append(jnp.exp2(s_f - (mx - far_bias)).astype(BF16))
            if n >= 1:
                parts.append(jnp.exp2(s_p - mx).astype(BF16))
            parts.append(jnp.exp2(s_d - mx).astype(BF16))
            p.append(jnp.concatenate(parts, axis=1) if len(parts) > 1 else parts[0])
        acc = _dot(jnp.concatenate(p, axis=0), _with_ones(v_ref[0:(n + 1) * t, :]))
        o = acc[:, :LANES] / acc[:, LANES:]
        o1, o2 = o[:t], o[t:]
        lf = lam_ref[...]
        lam = (jnp.exp(jnp.sum(lf[0:1] * lf[1:2], axis=-1, keepdims=True))
               - jnp.exp(jnp.sum(lf[2:3] * lf[3:4], axis=-1, keepdims=True)) + lam_init)
        o_ref[...] = _rms(o1 - lam * o2, subln_ref[...]) * (1.0 - lam_init)

    _per_q_tile(body)


def _da_attention(tab, lam_p, subln, aq, ak, av, da_bias, lam_init):
    smem = pl.BlockSpec(memory_space=pltpu.SMEM)
    return pl.pallas_call(
        functools.partial(_da_kernel, lam_init=lam_init),
        grid=(BATCH, DA_HEADS, ATT_NQ),
        in_specs=[smem, _const_spec(lam_p.shape), _const_spec(subln.shape),
                  pl.BlockSpec((ATT_TILE, LANES), lambda b, h, i: (b * ATT_NQ + i, h)),
                  pl.BlockSpec((SEQ, LANES), lambda b, h, i: (b, h)),
                  pl.BlockSpec((SEQ, LANES), lambda b, h, i: (b, h)),
                  pl.BlockSpec((1, 2, ATT_TILE, ATT_TILE), lambda b, h, i: (h, 0, 0, 0))],
        out_specs=pl.BlockSpec((ATT_TILE, LANES), lambda b, h, i: (b * ATT_NQ + i, h)),
        out_shape=jax.ShapeDtypeStruct((TOKENS, DA_WIDTH), F32),
        compiler_params=pltpu.CompilerParams(
            dimension_semantics=("parallel", "parallel", "parallel")),
        name="diff_attention",
    )(tab, lam_p, subln, aq, ak, av, da_bias)


def _mla_kernel(q_ref, k_ref, v_ref, o_ref):
    t = ATT_TILE
    lane = lax.broadcasted_iota(jnp.int32, (1, LANES), 1)
    row = lax.broadcasted_iota(jnp.int32, (t, t), 0)
    col = lax.broadcasted_iota(jnp.int32, (t, t), 1)

    def body(n):
        p = []
        for a in range(2):
            sl = slice(a * LANES, (a + 1) * LANES)
            q = q_ref[:, sl]
            s_d = jnp.where(row >= col, _dot_nt(q, k_ref[n * t:(n + 1) * t, sl]), NEG)
            mx = jnp.max(s_d, axis=-1, keepdims=True)
            parts = []
            if n >= 1:
                s_f = _dot_nt(q, k_ref[0:n * t, sl])
                mx = jnp.maximum(mx, jnp.max(s_f, axis=-1, keepdims=True))
                parts.append(jnp.exp2(s_f - mx).astype(BF16))
            parts.append(jnp.exp2(s_d - mx).astype(BF16))
            p.append(jnp.concatenate(parts, axis=1) if len(parts) > 1 else parts[0])
        acc = _dot(jnp.concatenate(p, axis=0), _with_ones(v_ref[0:(n + 1) * t, :]))
        o = acc[:, :LANES] / acc[:, LANES:]
        o_ref[...] = jnp.where(lane < MLA_V, o[:t], o[t:])

    _per_q_tile(body)


def _mla_attention(mq, mk, mv):
    return pl.pallas_call(
        _mla_kernel,
        grid=(BATCH, MLA_HEADS // 2, ATT_NQ),
        in_specs=[pl.BlockSpec((ATT_TILE, 2 * LANES), lambda b, h, i: (b * ATT_NQ + i, h)),
                  pl.BlockSpec((SEQ, 2 * LANES), lambda b, h, i: (b, h)),
                  pl.BlockSpec((SEQ, LANES), lambda b, h, i: (b, h))],
        out_specs=pl.BlockSpec((ATT_TILE, LANES), lambda b, h, i: (b * ATT_NQ + i, h)),
        out_shape=jax.ShapeDtypeStruct((TOKENS, MLA_WIDTH), F32),
        compiler_params=pltpu.CompilerParams(
            dimension_semantics=("parallel", "parallel", "parallel")),
        name="mla_attention",
    )(mq, mk, mv)


def _sw_kernel(sink_ref, q_ref, k_ref, v_ref, bias_ref, o_ref):
    qi = pl.program_id(1)
    row0 = pl.multiple_of(jnp.maximum(qi * SW_TILE - SW_WINDOW, 0), SW_WINDOW)
    k = k_ref[pl.ds(row0, SW_KEYS), :]
    v = v_ref[pl.ds(row0, SW_KEYS), :]
    lane = lax.broadcasted_iota(jnp.int32, (1, LANES), 1)
    lo = lane < HEAD_DIM
    zero = jnp.zeros_like(k)
    k_lo = [jnp.where(lo, k, zero), None]
    k_hi = [None, jnp.where(lo, zero, k)]
    k_hi[0] = pltpu.roll(k_lo[0], HEAD_DIM, 1)
    k_lo[1] = pltpu.roll(k_hi[1], HEAD_DIM, 1)
    v_swap = pltpu.roll(v, HEAD_DIM, 1)

    for hp in range(SW_HEADS // 2):
        kvh = hp // 2
        q = q_ref[:, hp * LANES:(hp + 1) * LANES]
        outs = []
        for a, kk in enumerate((k_lo[kvh], k_hi[kvh])):
            head = 2 * hp + a
            s = _dot_nt(q, kk) + bias_ref[0, head]
            sink = sink_ref[head]
            m = jnp.maximum(jnp.max(s, axis=-1, keepdims=True), sink)
            e = jnp.exp(s - m)
            l = jnp.sum(e, axis=-1, keepdims=True) + jnp.exp(sink - m)
            vv = v if (a == kvh) else v_swap
            outs.append(_dot(e.astype(BF16), vv) / l)
        o_ref[:, hp * LANES:(hp + 1) * LANES] = jnp.where(lo, outs[0], outs[1])


def _sw_attention(sinks, cq, ck, cv, sw_bias):
    smem = pl.BlockSpec(memory_space=pltpu.SMEM)
    return pl.pallas_call(
        _sw_kernel,
        grid=(BATCH, SW_NQ),
        in_specs=[smem,
                  pl.BlockSpec((SW_TILE, SW_WIDTH), lambda b, i: (b * SW_NQ + i, 0)),
                  pl.BlockSpec((SEQ, LANES), lambda b, i: (b, 0)),
                  pl.BlockSpec((SEQ, LANES), lambda b, i: (b, 0)),
                  pl.BlockSpec((1, SW_HEADS, SW_TILE, SW_KEYS),
                               lambda b, i: (jnp.minimum(i, 1), 0, 0, 0))],
        out_specs=pl.BlockSpec((SW_TILE, SW_WIDTH), lambda b, i: (b * SW_NQ + i, 0)),
        out_shape=jax.ShapeDtypeStruct((TOKENS, SW_WIDTH), F32),
        compiler_params=pltpu.CompilerParams(dimension_semantics=("parallel", "parallel")),
        name="sw_attention",
    )(sinks, cq, ck, cv, sw_bias)


def _merge_kernel(x_ref, p_ref, oa_ref, ob_ref, oc_ref, npre_ref, npost_ref, wz_ref, wg_ref,
                  wbr_ref, wout_ref, wpg_ref, wpp_ref, out_ref):
    x = x_ref[...]
    h = _rms(x, npre_ref[...]).astype(BF16)
    y = None
    for i, o_ref in enumerate((oa_ref, ob_ref, oc_ref)):
        z = _dot(h, wz_ref[:, i * 512:(i + 1) * 512])
        t = (o_ref[...] * (z * _sigmoid(z))).astype(BF16)
        br = _dot(t, wbr_ref[i])
        g = _sigmoid(_dot(h, wg_ref[:, i * D_MODEL:(i + 1) * D_MODEL]))
        y = g * br if y is None else y + g * br
    x1 = x + _rms(_dot(y.astype(BF16), wout_ref[...]), npost_ref[...])
    u = _dot(x1.astype(BF16), wpg_ref[...])
    pp = _dot(p_ref[...].astype(BF16), wpp_ref[...])
    out_ref[...] = x1 + _sigmoid(u) * pp


def _merge(x2, p2, oa, ob, oc, npre, npost, wz, wg, wbr, wout, wpg, wpp):
    nt = TOKENS // ROW_TILE
    row = lambda w: pl.BlockSpec((ROW_TILE, w), lambda i: (i, 0))
    return pl.pallas_call(
        _merge_kernel,
        grid=(nt,),
        in_specs=[row(D_MODEL), row(PLE_DIM), row(512), row(512), row(512),
                  _const_spec(npre.shape), _const_spec(npost.shape), _const_spec(wz.shape),
                  _const_spec(wg.shape), _const_spec(wbr.shape), _const_spec(wout.shape),
                  _const_spec(wpg.shape), _const_spec(wpp.shape)],
        out_specs=row(D_MODEL),
        out_shape=jax.ShapeDtypeStruct((TOKENS, D_MODEL), F32),
        compiler_params=pltpu.CompilerParams(dimension_semantics=("parallel",)),
        name="gated_merge",
    )(x2, p2, oa, ob, oc, npre, npost, wz, wg, wbr, wout, wpg, wpp)


def _layer_weights(w_in_l, mla_w_qb_l, mla_w_kvb_l):
    col = lambda i: w_in_l[:, _OFF[i]:_OFF[i + 1]]
    zeros = lambda n: jnp.zeros((D_MODEL, n), w_in_l.dtype)
    kr_slab = jnp.concatenate([zeros(MLA_NOPE), col(6), zeros(LANES - MLA_NOPE - MLA_ROPE)], 1)
    w1 = jnp.concatenate([col(0), col(1), col(2), col(8), col(9), col(10), col(4), col(5),
                          kr_slab], axis=1).astype(BF16)
    wz = jnp.concatenate([col(3), col(7), col(11)], axis=1).astype(BF16)
    wg = col(12).astype(BF16)
    wqb = mla_w_qb_l.reshape(MLA_Q_RANK, MLA_HEADS, MLA_NOPE + MLA_ROPE)
    wqb = jnp.pad(wqb, ((0, 0), (0, 0), (0, LANES - MLA_NOPE - MLA_ROPE)))
    wqb = wqb.reshape(MLA_Q_RANK, MLA_HEADS * LANES).astype(BF16)
    wkv = mla_w_kvb_l.reshape(MLA_KV_RANK, MLA_HEADS, MLA_NOPE + MLA_V)
    wk = jnp.pad(wkv[:, :, :MLA_NOPE], ((0, 0), (0, 0), (0, LANES - MLA_NOPE)))
    wk = wk.reshape(MLA_KV_RANK, MLA_HEADS * LANES).astype(BF16)
    wv = wkv[:, :, MLA_NOPE:].reshape(MLA_KV_RANK, MLA_WIDTH).astype(BF16)
    return w1, wz, wg, wqb, wk, wv


def kernel(x, p, positions, rel_bias, norm_pre, norm_post, w_in, da_lambda, da_subln, mla_q_norm,
           mla_w_qb, mla_kv_norm, mla_w_kvb, sw_sinks, w_br_a, w_br_b, w_br_c, w_out, w_ple_gate,
           w_ple_proj):
    assert x.shape == (BATCH, SEQ, D_MODEL) and p.shape == (DEPTH, BATCH, SEQ, PLE_DIM)
    x2 = x.reshape(TOKENS, D_MODEL).astype(F32)
    pos2 = positions.reshape(TOKENS, 1).astype(jnp.int32)
    half = MLA_ROPE // 2
    inv_freq = ROPE_THETA ** (-jnp.arange(half, dtype=F32) / half)
    invf = jnp.zeros((1, LANES), F32)
    invf = invf.at[0, MLA_NOPE:MLA_NOPE + half].set(inv_freq)
    invf = invf.at[0, MLA_NOPE + half:MLA_NOPE + 2 * half].set(inv_freq)
    tab, da_bias, sw_bias = _bias_tiles(rel_bias)

    for l in range(DEPTH):
        w1, wz, wg, wqb, wk, wv = _layer_weights(w_in[l], mla_w_qb[l], mla_w_kvb[l])
        npre = norm_pre[l].reshape(1, D_MODEL).astype(F32)
        npost = norm_post[l].reshape(1, D_MODEL).astype(F32)
        aq, ak, av, cq, ck, cv, mq, mk, mv = _inproj(
            x2, pos2, npre, w1, mla_q_norm[l].reshape(1, MLA_Q_RANK).astype(F32), wqb,
            mla_kv_norm[l].reshape(1, MLA_KV_RANK).astype(F32), wk, wv, invf)
        lam_init = 0.8 - 0.6 * math.exp(-0.3 * l)
        oa = _da_attention(tab, da_lambda[l].astype(F32), da_subln[l].reshape(1, DA_V).astype(F32),
                           aq, ak, av, da_bias, lam_init)
        ob = _mla_attention(mq, mk, mv)
        oc = _sw_attention(sw_sinks[l].astype(F32), cq, ck, cv, sw_bias)
        wbr = jnp.stack([w_br_a[l], w_br_b[l], w_br_c[l]]).astype(BF16)
        x2 = _merge(x2, p[l].reshape(TOKENS, PLE_DIM).astype(F32), oa, ob, oc, npre, npost, wz, wg,
                    wbr, w_out[l].astype(BF16), w_ple_gate[l].astype(BF16),
                    w_ple_proj[l].astype(BF16))
    return x2.reshape(BATCH, SEQ, D_MODEL)
```

```python
import functools
import math

import numpy as np
import jax
import jax.numpy as jnp
from jax import lax
from jax.experimental import pallas as pl
from jax.experimental.pallas import tpu as pltpu

D_MODEL = 1024
BATCH = 8
SEQ = 2048
DEPTH = 2
PLE_DIM = 256
HEAD_DIM = 64
DA_HEADS = 4
DA_V = 2 * HEAD_DIM
DA_QK = DA_HEADS * 2 * HEAD_DIM
DA_WIDTH = DA_HEADS * DA_V
MLA_HEADS = 8
MLA_NOPE = 64
MLA_ROPE = 32
MLA_V = 64
MLA_Q_RANK = 256
MLA_KV_RANK = 128
MLA_WIDTH = MLA_HEADS * MLA_V
SW_HEADS = 8
SW_KV_HEADS = 2
SW_WINDOW = 128
SW_WIDTH = SW_HEADS * HEAD_DIM
REL_BUCKETS = 32
REL_MAX_DIST = 128
BIAS_HEADS = DA_HEADS + SW_HEADS
ROPE_THETA = 10000.0
EPS = 1e-6
NEG = -1e30

IN_SIZES = (DA_QK, DA_QK, DA_WIDTH, DA_WIDTH,
            MLA_Q_RANK, MLA_KV_RANK, MLA_ROPE, MLA_WIDTH,
            SW_HEADS * HEAD_DIM, SW_KV_HEADS * HEAD_DIM, SW_KV_HEADS * HEAD_DIM, SW_WIDTH,
            3 * D_MODEL)
_OFF = tuple(int(c) for c in np.cumsum((0,) + IN_SIZES))

LANES = 128
TOKENS = BATCH * SEQ
ROW_TILE = 512
ATT_TILE = 512
ATT_NQ = SEQ // ATT_TILE
SW_GROUP = SW_HEADS // SW_KV_HEADS
SW_SUB = SW_WINDOW
SW_KEYS = SW_SUB + SW_WINDOW
SW_SUBS = 2
SW_TILE = SW_SUBS * SW_SUB
SW_NQ = SEQ // SW_TILE
SW_ROWS = SW_GROUP * SW_SUB
LOG2E = math.log2(math.e)
DA_SCALE = HEAD_DIM ** -0.5 * LOG2E
SW_SCALE = HEAD_DIM ** -0.5 * LOG2E
MLA_SCALE = (MLA_NOPE + MLA_ROPE) ** -0.5 * LOG2E
BF16 = jnp.bfloat16
F32 = jnp.float32

_W1_SIZES = (DA_QK, DA_QK, DA_WIDTH, SW_WIDTH, 2 * LANES, MLA_Q_RANK, 2 * LANES)
_W1_OFF = tuple(int(c) for c in np.cumsum((0,) + _W1_SIZES))


def _bucket_thresholds():
    max_exact = REL_BUCKETS // 2
    n = np.arange(0, REL_MAX_DIST + 1)
    nf = np.maximum(n, 1).astype(np.float64)
    large = max_exact + (np.log(nf / max_exact) / math.log(REL_MAX_DIST / max_exact)
                         * (REL_BUCKETS - max_exact)).astype(np.int64)
    bucket = np.where(n < max_exact, n, np.minimum(large, REL_BUCKETS - 1))
    return tuple(int(np.argmax(bucket >= j)) for j in range(REL_BUCKETS))


_THRESH = _bucket_thresholds()


def _dot(a, b):
    return jnp.dot(a, b, preferred_element_type=F32)


def _dot_nt(a, b):
    return lax.dot_general(a, b, (((1,), (1,)), ((), ())), preferred_element_type=F32)


def _rms(x, w):
    y = x * lax.rsqrt(jnp.mean(x * x, axis=-1, keepdims=True) + EPS)
    return y * w


def _sigmoid(x):
    return 1.0 / (1.0 + jnp.exp(-x))


def _const_spec(shape):
    nd = len(shape)
    return pl.BlockSpec(shape, lambda *_: (0,) * nd, pipeline_mode=pl.Buffered(1))


def _bias_of(rel, tab_ref, head):
    b = jnp.full(rel.shape, tab_ref[head], F32)
    for j in range(1, REL_BUCKETS):
        b = jnp.where(rel >= _THRESH[j], tab_ref[j * BIAS_HEADS + head], b)
    return b


def _da_bias_kernel(tab_ref, out_ref):
    h = pl.program_id(0)
    i = lax.broadcasted_iota(jnp.int32, (ATT_TILE, ATT_TILE), 0)
    j = lax.broadcasted_iota(jnp.int32, (ATT_TILE, ATT_TILE), 1)
    rel = i - j
    out_ref[0, 0] = jnp.where(rel >= 0, _bias_of(rel, tab_ref, h) * LOG2E, NEG)
    out_ref[0, 1] = _bias_of(rel + ATT_TILE, tab_ref, h) * LOG2E


def _sw_bias_kernel(tab_ref, out_ref):
    t = pl.program_id(0)
    g = pl.program_id(1)
    i = lax.broadcasted_iota(jnp.int32, (SW_SUB, SW_KEYS), 0)
    j = lax.broadcasted_iota(jnp.int32, (SW_SUB, SW_KEYS), 1)
    rel = i - j + t * SW_WINDOW
    ok = (rel >= 0) & (rel < SW_WINDOW)
    for a in range(SW_GROUP):
        head = DA_HEADS + g * SW_GROUP + a
        out_ref[0, 0, a * SW_SUB:(a + 1) * SW_SUB, :] = jnp.where(
            ok, _bias_of(rel, tab_ref, head) * LOG2E, NEG)


def _bias_tiles(rel_bias):
    tab = rel_bias.astype(F32).reshape(REL_BUCKETS * BIAS_HEADS)
    smem = pl.BlockSpec(memory_space=pltpu.SMEM)
    da = pl.pallas_call(
        _da_bias_kernel,
        grid=(DA_HEADS,),
        in_specs=[smem],
        out_specs=pl.BlockSpec((1, 2, ATT_TILE, ATT_TILE), lambda h: (h, 0, 0, 0)),
        out_shape=jax.ShapeDtypeStruct((DA_HEADS, 2, ATT_TILE, ATT_TILE), F32),
        name="da_bias_tiles",
    )(tab)
    sw = pl.pallas_call(
        _sw_bias_kernel,
        grid=(2, SW_KV_HEADS),
        in_specs=[smem],
        out_specs=pl.BlockSpec((1, 1, SW_ROWS, SW_KEYS), lambda t, g: (t, g, 0, 0)),
        out_shape=jax.ShapeDtypeStruct((2, SW_KV_HEADS, SW_ROWS, SW_KEYS), F32),
        name="sw_bias_tiles",
    )(tab)
    return tab, da, sw


def _rope_table_kernel(pos_ref, invf_ref, cos_ref, sin_a_ref, sin_b_ref):
    ang = pos_ref[...].astype(F32) * invf_ref[...]
    sin = jnp.sin(ang)
    lane = lax.broadcasted_iota(jnp.int32, (1, LANES), 1)
    first = (lane >= MLA_NOPE) & (lane < MLA_NOPE + MLA_ROPE // 2)
    second = (lane >= MLA_NOPE + MLA_ROPE // 2) & (lane < MLA_NOPE + MLA_ROPE)
    cos_ref[...] = jnp.cos(ang)
    sin_a_ref[...] = jnp.where(first, -sin, 0.0)
    sin_b_ref[...] = jnp.where(second, sin, 0.0)


def _rope_tables(pos2, invf):
    row = lambda w: pl.BlockSpec((ROW_TILE, w), lambda i: (i, 0))
    return pl.pallas_call(
        _rope_table_kernel,
        grid=(TOKENS // ROW_TILE,),
        in_specs=[row(1), _const_spec(invf.shape)],
        out_specs=[row(LANES)] * 3,
        out_shape=[jax.ShapeDtypeStruct((TOKENS, LANES), F32)] * 3,
        compiler_params=pltpu.CompilerParams(dimension_semantics=("parallel",)),
        name="rope_tables",
    )(pos2, invf)


def _inproj_kernel(x_ref, cos_ref, sin_a_ref, sin_b_ref, npre_ref, w1_ref, qn_ref, wqb_ref,
                   kvn_ref, wk_ref, wv_ref, aq_ref, ak_ref, av_ref, cq_ref, ck_ref, cv_ref,
                   mq_ref, mk_ref, mv_ref):
    h = _rms(x_ref[...], npre_ref[...]).astype(BF16)

    def proj(i):
        return _dot(h, w1_ref[:, _W1_OFF[i]:_W1_OFF[i + 1]])

    cos = cos_ref[...]
    sin_a = sin_a_ref[...]
    sin_b = sin_b_ref[...]

    def rope(t):
        return (t * cos + pltpu.roll(t, LANES - MLA_ROPE // 2, 1) * sin_a
                + pltpu.roll(t, MLA_ROPE // 2, 1) * sin_b)

    bq = proj(5)
    bkv_kr = proj(6)
    bkv = bkv_kr[:, :MLA_KV_RANK]
    kr = rope(bkv_kr[:, MLA_KV_RANK:])
    qm = _dot(_rms(bq, qn_ref[...]).astype(BF16), wqb_ref[...])
    kvn = _rms(bkv, kvn_ref[...]).astype(BF16)
    km = _dot(kvn, wk_ref[...])
    mv_ref[...] = _dot(kvn, wv_ref[...]).astype(BF16)
    for hh in range(MLA_HEADS):
        sl = slice(hh * LANES, (hh + 1) * LANES)
        mq_ref[:, sl] = (rope(qm[:, sl]) * MLA_SCALE).astype(BF16)
        mk_ref[:, sl] = (km[:, sl] + kr).astype(BF16)

    aq_ref[...] = (proj(0) * DA_SCALE).astype(BF16)
    ak_ref[...] = proj(1).astype(BF16)
    av_ref[...] = proj(2).astype(BF16)
    cq_ref[...] = (proj(3) * SW_SCALE).astype(BF16)
    ckv = proj(4)
    ck_ref[...] = ckv[:, :LANES].astype(BF16)
    cv_ref[...] = ckv[:, LANES:].astype(BF16)


def _inproj(x2, rope_tabs, npre, w1, qn, wqb, kvn, wk, wv):
    row = lambda w: pl.BlockSpec((ROW_TILE, w), lambda i: (i, 0))
    widths = (DA_QK, DA_QK, DA_WIDTH, SW_WIDTH, LANES, LANES, MLA_HEADS * LANES,
              MLA_HEADS * LANES, MLA_WIDTH)
    return pl.pallas_call(
        _inproj_kernel,
        grid=(TOKENS // ROW_TILE,),
        in_specs=[row(D_MODEL), row(LANES), row(LANES), row(LANES), _const_spec(npre.shape),
                  _const_spec(w1.shape), _const_spec(qn.shape), _const_spec(wqb.shape),
                  _const_spec(kvn.shape), _const_spec(wk.shape), _const_spec(wv.shape)],
        out_specs=[row(w) for w in widths],
        out_shape=[jax.ShapeDtypeStruct((TOKENS, w), BF16) for w in widths],
        compiler_params=pltpu.CompilerParams(dimension_semantics=("parallel",)),
        name="inproj",
    )(x2, *rope_tabs, npre, w1, qn, wqb, kvn, wk, wv)


def _with_ones(v):
    return jnp.concatenate([v, jnp.ones(v.shape, v.dtype)], axis=1)


def _per_q_tile(body):
    qi = pl.program_id(2)
    for n in range(ATT_NQ):
        pl.when(qi == n)(functools.partial(body, n))


def _da_kernel(tab_ref, lam_ref, subln_ref, q_ref, k_ref, v_ref, bias_ref, o_ref, *, lam_init):
    t = ATT_TILE
    h = pl.program_id(1)
    far_bias = tab_ref[(REL_BUCKETS - 1) * BIAS_HEADS + h] * LOG2E
    lane = lax.broadcasted_iota(jnp.int32, (1, LANES), 1)
    lo = lane < HEAD_DIM

    def body(n):
        q = q_ref[...]
        zero = jnp.zeros_like(q)
        p = []
        for qh in (jnp.where(lo, q, zero), jnp.where(lo, zero, q)):
            s_d = _dot_nt(qh, k_ref[n * t:(n + 1) * t, :]) + bias_ref[0, 0]
            mx = jnp.max(s_d, axis=-1, keepdims=True)
            if n >= 1:
                s_p = _dot_nt(qh, k_ref[(n - 1) * t:n * t, :]) + bias_ref[0, 1]
                mx = jnp.maximum(mx, jnp.max(s_p, axis=-1, keepdims=True))
            if n >= 2:
                s_f = _dot_nt(qh, k_ref[0:(n - 1) * t, :])
                mx = jnp.maximum(mx, jnp.max(s_f, axis=-1, keepdims=True) + far_bias)
            parts = []
            if n >= 2:
                parts.append(jnp.exp2(s_f - (mx - far_bias)).astype(BF16))
            if n >= 1:
                parts.append(jnp.exp2(s_p - mx).astype(BF16))
            parts.append(jnp.exp2(s_d - mx).astype(BF16))
            p.append(jnp.concatenate(parts, axis=1) if len(parts) > 1 else parts[0])
        acc = _dot(jnp.concatenate(p, axis=0), _with_ones(v_ref[0:(n + 1) * t, :]))
        o = acc[:, :LANES] / acc[:, LANES:]
        o1, o2 = o[:t], o[t:]
        lf = lam_ref[...]
        lam = (jnp.exp(jnp.sum(lf[0:1] * lf[1:2], axis=-1, keepdims=True))
               - jnp.exp(jnp.sum(lf[2:3] * lf[3:4], axis=-1, keepdims=True)) + lam_init)
        o_ref[...] = _rms(o1 - lam * o2, subln_ref[...]) * (1.0 - lam_init)

    _per_q_tile(body)


def _da_attention(tab, lam_p, subln, aq, ak, av, da_bias, lam_init):
    smem = pl.BlockSpec(memory_space=pltpu.SMEM)
    return pl.pallas_call(
        functools.partial(_da_kernel, lam_init=lam_init),
        grid=(BATCH, DA_HEADS, ATT_NQ),
        in_specs=[smem, _const_spec(lam_p.shape), _const_spec(subln.shape),
                  pl.BlockSpec((ATT_TILE, LANES), lambda b, h, i: (b * ATT_NQ + i, h)),
                  pl.BlockSpec((SEQ, LANES), lambda b, h, i: (b, h)),
                  pl.BlockSpec((SEQ, LANES), lambda b, h, i: (b, h)),
                  pl.BlockSpec((1, 2, ATT_TILE, ATT_TILE), lambda b, h, i: (h, 0, 0, 0))],
        out_specs=pl.BlockSpec((ATT_TILE, LANES), lambda b, h, i: (b * ATT_NQ + i, h)),
        out_shape=jax.ShapeDtypeStruct((TOKENS, DA_WIDTH), F32),
        compiler_params=pltpu.CompilerParams(
            dimension_semantics=("parallel", "parallel", "parallel")),
        name="diff_attention",
    )(tab, lam_p, subln, aq, ak, av, da_bias)


def _mla_kernel(q_ref, k_ref, v_ref, o_ref):
    t = ATT_TILE
    lane = lax.broadcasted_iota(jnp.int32, (1, LANES), 1)
    row = lax.broadcasted_iota(jnp.int32, (t, t), 0)
    col = lax.broadcasted_iota(jnp.int32, (t, t), 1)

    def body(n):
        p = []
        for a in range(2):
            sl = slice(a * LANES, (a + 1) * LANES)
            q = q_ref[:, sl]
            s_d = jnp.where(row >= col, _dot_nt(q, k_ref[n * t:(n + 1) * t, sl]), NEG)
            mx = jnp.max(s_d, axis=-1, keepdims=True)
            parts = []
            if n >= 1:
                s_f = _dot_nt(q, k_ref[0:n * t, sl])
                mx = jnp.maximum(mx, jnp.max(s_f, axis=-1, keepdims=True))
                parts.append(jnp.exp2(s_f - mx).astype(BF16))
            parts.append(jnp.exp2(s_d - mx).astype(BF16))
            p.append(jnp.concatenate(parts, axis=1) if len(parts) > 1 else parts[0])
        acc = _dot(jnp.concatenate(p, axis=0), _with_ones(v_ref[0:(n + 1) * t, :]))
        o = acc[:, :LANES] / acc[:, LANES:]
        o_ref[...] = jnp.where(lane < MLA_V, o[:t], o[t:])

    _per_q_tile(body)


def _mla_attention(mq, mk, mv):
    return pl.pallas_call(
        _mla_kernel,
        grid=(BATCH, MLA_HEADS // 2, ATT_NQ),
        in_specs=[pl.BlockSpec((ATT_TILE, 2 * LANES), lambda b, h, i: (b * ATT_NQ + i, h)),
                  pl.BlockSpec((SEQ, 2 * LANES), lambda b, h, i: (b, h)),
                  pl.BlockSpec((SEQ, LANES), lambda b, h, i: (b, h))],
        out_specs=pl.BlockSpec((ATT_TILE, LANES), lambda b, h, i: (b * ATT_NQ + i, h)),
        out_shape=jax.ShapeDtypeStruct((TOKENS, MLA_WIDTH), F32),
        compiler_params=pltpu.CompilerParams(
            dimension_semantics=("parallel", "parallel", "parallel")),
        name="mla_attention",
    )(mq, mk, mv)


def _sw_kernel(sink_ref, q_ref, k_ref, v_ref, bias_ref, o_ref):
    qi = pl.program_id(1)
    lane = lax.broadcasted_iota(jnp.int32, (1, LANES), 1)
    lo = lane < HEAD_DIM
    head_of_row = lax.broadcasted_iota(jnp.int32, (SW_ROWS, 1), 0) // SW_SUB

    for sub in range(SW_SUBS):
        r0 = qi * SW_TILE + sub * SW_SUB
        first = r0 == 0
        key0 = pl.multiple_of(jnp.maximum(r0 - SW_WINDOW, 0), SW_WINDOW)
        k = k_ref[pl.ds(key0, SW_KEYS), :]
        v = v_ref[pl.ds(key0, SW_KEYS), :]
        k_sw = pltpu.roll(k, HEAD_DIM, 1)
        v_sw = pltpu.roll(v, HEAD_DIM, 1)
        variant = jnp.where(first, 0, 1)
        for g in range(SW_KV_HEADS):
            kg = jnp.where(lo, k, k_sw) if g == 0 else jnp.where(lo, k_sw, k)
            vg = jnp.where(lo, v, v_sw) if g == 0 else jnp.where(lo, v_sw, v)
            qs = []
            for slab in range(2):
                c0 = (2 * g + slab) * LANES
                q = q_ref[sub * SW_SUB:(sub + 1) * SW_SUB, c0:c0 + LANES]
                zero = jnp.zeros_like(q)
                qs += [jnp.where(lo, q, zero), jnp.where(lo, zero, q)]
            s = _dot_nt(jnp.concatenate(qs, axis=0), kg) + bias_ref[variant, g]
            sink = jnp.zeros((SW_ROWS, 1), F32)
            for a in range(SW_GROUP):
                sink = jnp.where(head_of_row == a, sink_ref[g * SW_GROUP + a] * LOG2E, sink)
            m = jnp.maximum(jnp.max(s, axis=-1, keepdims=True), sink)
            acc = _dot(jnp.exp2(s - m).astype(BF16), _with_ones(vg))
            o = acc[:, :LANES] / (acc[:, LANES:] + jnp.exp2(sink - m))
            for slab in range(2):
                c0 = (2 * g + slab) * LANES
                ra = 2 * slab * SW_SUB
                o_ref[sub * SW_SUB:(sub + 1) * SW_SUB, c0:c0 + LANES] = jnp.where(
                    lo, o[ra:ra + SW_SUB], o[ra + SW_SUB:ra + 2 * SW_SUB])


def _sw_attention(sinks, cq, ck, cv, sw_bias):
    smem = pl.BlockSpec(memory_space=pltpu.SMEM)
    return pl.pallas_call(
        _sw_kernel,
        grid=(BATCH, SW_NQ),
        in_specs=[smem,
                  pl.BlockSpec((SW_TILE, SW_WIDTH), lambda b, i: (b * SW_NQ + i, 0)),
                  pl.BlockSpec((SEQ, LANES), lambda b, i: (b, 0)),
                  pl.BlockSpec((SEQ, LANES), lambda b, i: (b, 0)),
                  _const_spec(sw_bias.shape)],
        out_specs=pl.BlockSpec((SW_TILE, SW_WIDTH), lambda b, i: (b * SW_NQ + i, 0)),
        out_shape=jax.ShapeDtypeStruct((TOKENS, SW_WIDTH), F32),
        compiler_params=pltpu.CompilerParams(dimension_semantics=("parallel", "parallel")),
        name="sw_attention",
    )(sinks, cq, ck, cv, sw_bias)


def _merge_kernel(x_ref, p_ref, oa_ref, ob_ref, oc_ref, npre_ref, npost_ref, wz_ref, wg_ref,
                  wbr_ref, wout_ref, wpg_ref, wpp_ref, out_ref):
    x = x_ref[...]
    h = _rms(x, npre_ref[...]).astype(BF16)
    y = None
    for i, o_ref in enumerate((oa_ref, ob_ref, oc_ref)):
        z = _dot(h, wz_ref[:, i * 512:(i + 1) * 512])
        t = (o_ref[...] * (z * _sigmoid(z))).astype(BF16)
        br = _dot(t, wbr_ref[i])
        g = _sigmoid(_dot(h, wg_ref[:, i * D_MODEL:(i + 1) * D_MODEL]))
        y = g * br if y is None else y + g * br
    x1 = x + _rms(_dot(y.astype(BF16), wout_ref[...]), npost_ref[...])
    u = _dot(x1.astype(BF16), wpg_ref[...])
    pp = _dot(p_ref[...].astype(BF16), wpp_ref[...])
    out_ref[...] = x1 + _sigmoid(u) * pp


def _merge(x2, p2, oa, ob, oc, npre, npost, wz, wg, wbr, wout, wpg, wpp):
    nt = TOKENS // ROW_TILE
    row = lambda w: pl.BlockSpec((ROW_TILE, w), lambda i: (i, 0))
    return pl.pallas_call(
        _merge_kernel,
        grid=(nt,),
        in_specs=[row(D_MODEL), row(PLE_DIM), row(512), row(512), row(512),
                  _const_spec(npre.shape), _const_spec(npost.shape), _const_spec(wz.shape),
                  _const_spec(wg.shape), _const_spec(wbr.shape), _const_spec(wout.shape),
                  _const_spec(wpg.shape), _const_spec(wpp.shape)],
        out_specs=row(D_MODEL),
        out_shape=jax.ShapeDtypeStruct((TOKENS, D_MODEL), F32),
        compiler_params=pltpu.CompilerParams(dimension_semantics=("parallel",)),
        name="gated_merge",
    )(x2, p2, oa, ob, oc, npre, npost, wz, wg, wbr, wout, wpg, wpp)


def _layer_weights(w_in_l, mla_w_qb_l, mla_w_kvb_l):
    col = lambda i: w_in_l[:, _OFF[i]:_OFF[i + 1]]
    zeros = lambda n: jnp.zeros((D_MODEL, n), w_in_l.dtype)
    kr_slab = jnp.concatenate([zeros(MLA_NOPE), col(6), zeros(LANES - MLA_NOPE - MLA_ROPE)], 1)
    w1 = jnp.concatenate([col(0), col(1), col(2), col(8), col(9), col(10), col(4), col(5),
                          kr_slab], axis=1).astype(BF16)
    wz = jnp.concatenate([col(3), col(7), col(11)], axis=1).astype(BF16)
    wg = col(12).astype(BF16)
    wqb = mla_w_qb_l.reshape(MLA_Q_RANK, MLA_HEADS, MLA_NOPE + MLA_ROPE)
    wqb = jnp.pad(wqb, ((0, 0), (0, 0), (0, LANES - MLA_NOPE - MLA_ROPE)))
    wqb = wqb.reshape(MLA_Q_RANK, MLA_HEADS * LANES).astype(BF16)
    wkv = mla_w_kvb_l.reshape(MLA_KV_RANK, MLA_HEADS, MLA_NOPE + MLA_V)
    wk = jnp.pad(wkv[:, :, :MLA_NOPE], ((0, 0), (0, 0), (0, LANES - MLA_NOPE)))
    wk = wk.reshape(MLA_KV_RANK, MLA_HEADS * LANES).astype(BF16)
    wv = wkv[:, :, MLA_NOPE:].reshape(MLA_KV_RANK, MLA_WIDTH).astype(BF16)
    return w1, wz, wg, wqb, wk, wv


def kernel(x, p, positions, rel_bias, norm_pre, norm_post, w_in, da_lambda, da_subln, mla_q_norm,
           mla_w_qb, mla_kv_norm, mla_w_kvb, sw_sinks, w_br_a, w_br_b, w_br_c, w_out, w_ple_gate,
           w_ple_proj):
    assert x.shape == (BATCH, SEQ, D_MODEL) and p.shape == (DEPTH, BATCH, SEQ, PLE_DIM)
    x2 = x.reshape(TOKENS, D_MODEL).astype(F32)
    pos2 = positions.reshape(TOKENS, 1).astype(jnp.int32)
    half = MLA_ROPE // 2
    inv_freq = ROPE_THETA ** (-jnp.arange(half, dtype=F32) / half)
    invf = jnp.zeros((1, LANES), F32)
    invf = invf.at[0, MLA_NOPE:MLA_NOPE + half].set(inv_freq)
    invf = invf.at[0, MLA_NOPE + half:MLA_NOPE + 2 * half].set(inv_freq)
    rope_tabs = _rope_tables(pos2, invf)
    tab, da_bias, sw_bias = _bias_tiles(rel_bias)

    for l in range(DEPTH):
        w1, wz, wg, wqb, wk, wv = _layer_weights(w_in[l], mla_w_qb[l], mla_w_kvb[l])
        npre = norm_pre[l].reshape(1, D_MODEL).astype(F32)
        npost = norm_post[l].reshape(1, D_MODEL).astype(F32)
        aq, ak, av, cq, ck, cv, mq, mk, mv = _inproj(
            x2, rope_tabs, npre, w1, mla_q_norm[l].reshape(1, MLA_Q_RANK).astype(F32), wqb,
            mla_kv_norm[l].reshape(1, MLA_KV_RANK).astype(F32), wk, wv)
        lam_init = 0.8 - 0.6 * math.exp(-0.3 * l)
        oa = _da_attention(tab, da_lambda[l].astype(F32), da_subln[l].reshape(1, DA_V).astype(F32),
                           aq, ak, av, da_bias, lam_init)
        ob = _mla_attention(mq, mk, mv)
        oc = _sw_attention(sw_sinks[l].astype(F32), cq, ck, cv, sw_bias)
        wbr = jnp.stack([w_br_a[l], w_br_b[l], w_br_c[l]]).astype(BF16)
        x2 = _merge(x2, p[l].reshape(TOKENS, PLE_DIM).astype(F32), oa, ob, oc, npre, npost, wz, wg,
                    wbr, w_out[l].astype(BF16), w_ple_gate[l].astype(BF16),
                    w_ple_proj[l].astype(BF16))
    return x2.reshape(BATCH, SEQ, D_MODEL)
```

```python
import functools
import math

import numpy as np
import jax
import jax.numpy as jnp
from jax import lax
from jax.experimental import pallas as pl
from jax.experimental.pallas import tpu as pltpu

D_MODEL = 1024
BATCH = 8
SEQ = 2048
DEPTH = 2
PLE_DIM = 256
HEAD_DIM = 64
DA_HEADS = 4
DA_V = 2 * HEAD_DIM
DA_QK = DA_HEADS * 2 * HEAD_DIM
DA_WIDTH = DA_HEADS * DA_V
MLA_HEADS = 8
MLA_NOPE = 64
MLA_ROPE = 32
MLA_V = 64
MLA_Q_RANK = 256
MLA_KV_RANK = 128
MLA_WIDTH = MLA_HEADS * MLA_V
SW_HEADS = 8
SW_KV_HEADS = 2
SW_WINDOW = 128
SW_WIDTH = SW_HEADS * HEAD_DIM
N_BRANCHES = 3
REL_BUCKETS = 32
REL_MAX_DIST = 128
BIAS_HEADS = DA_HEADS + SW_HEADS
ROPE_THETA = 10000.0
EPS = 1e-6
NEG = -1e30

IN_SIZES = (DA_QK, DA_QK, DA_WIDTH, DA_WIDTH,
            MLA_Q_RANK, MLA_KV_RANK, MLA_ROPE, MLA_WIDTH,
            SW_HEADS * HEAD_DIM, SW_KV_HEADS * HEAD_DIM, SW_KV_HEADS * HEAD_DIM, SW_WIDTH,
            N_BRANCHES * D_MODEL)
_OFF = tuple(int(c) for c in np.cumsum((0,) + IN_SIZES))

LANES = 128
TOKENS = BATCH * SEQ
ROW_TILE = 512
ATT_TILE = 512
ATT_NQ = SEQ // ATT_TILE
ATT_STEPS = tuple((n,) for n in range(ATT_NQ))
SW_GROUP = SW_HEADS // SW_KV_HEADS
SW_SUB = SW_WINDOW
SW_KEYS = SW_SUB + SW_WINDOW
SW_SUBS = 2
SW_TILE = SW_SUBS * SW_SUB
SW_NQ = SEQ // SW_TILE
SW_ROWS = SW_GROUP * SW_SUB
LOG2E = math.log2(math.e)
DA_SCALE = HEAD_DIM ** -0.5 * LOG2E
SW_SCALE = HEAD_DIM ** -0.5 * LOG2E
MLA_SCALE = (MLA_NOPE + MLA_ROPE) ** -0.5 * LOG2E
BF16 = jnp.bfloat16
F32 = jnp.float32

_LO_END = _OFF[6] + LANES
_HI0 = _OFF[7]
_HI_INPROJ = _OFF[11] - _HI0
assert _OFF[6] % LANES == 0 and all((_OFF[i] - _HI0) % LANES == 0 for i in range(7, 13))
assert _OFF[3] % DA_WIDTH == 0


def _hi(i):
    return _OFF[i] - _HI0


def _bucket_thresholds():
    max_exact = REL_BUCKETS // 2
    n = np.arange(0, REL_MAX_DIST + 1)
    nf = np.maximum(n, 1).astype(np.float64)
    large = max_exact + (np.log(nf / max_exact) / math.log(REL_MAX_DIST / max_exact)
                         * (REL_BUCKETS - max_exact)).astype(np.int64)
    bucket = np.where(n < max_exact, n, np.minimum(large, REL_BUCKETS - 1))
    return tuple(int(np.argmax(bucket >= j)) for j in range(REL_BUCKETS))


_THRESH = _bucket_thresholds()


def _dot(a, b):
    return jnp.dot(a, b, preferred_element_type=F32)


def _dot_nt(a, b):
    return lax.dot_general(a, b, (((1,), (1,)), ((), ())), preferred_element_type=F32)


def _rms(x, w):
    y = x * lax.rsqrt(jnp.mean(x * x, axis=-1, keepdims=True) + EPS)
    return y * w


def _sigmoid(x):
    return 1.0 / (1.0 + jnp.exp(-x))


def _const_spec(shape):
    nd = len(shape)
    return pl.BlockSpec(shape, lambda *_: (0,) * nd, pipeline_mode=pl.Buffered(1))


def _layer_spec(arr, l, cols=None, col_block=0):
    shape = (1,) + arr.shape[1:-1] + (arr.shape[-1] if cols is None else cols,)
    index = (l,) + (0,) * (len(shape) - 2) + (col_block,)
    return pl.BlockSpec(shape, lambda *_: index, pipeline_mode=pl.Buffered(1))


def _row_spec(width):
    return pl.BlockSpec((ROW_TILE, width), lambda i: (i, 0))


def _rope_table_kernel(pos_ref, invf_ref, cos_ref, sin_a_ref, sin_b_ref):
    ang = pos_ref[...].astype(F32) * invf_ref[...]
    sin = jnp.sin(ang)
    lane = lax.broadcasted_iota(jnp.int32, (1, LANES), 1)
    first = (lane >= MLA_NOPE) & (lane < MLA_NOPE + MLA_ROPE // 2)
    second = (lane >= MLA_NOPE + MLA_ROPE // 2) & (lane < MLA_NOPE + MLA_ROPE)
    cos_ref[...] = jnp.cos(ang)
    sin_a_ref[...] = jnp.where(first, -sin, 0.0)
    sin_b_ref[...] = jnp.where(second, sin, 0.0)


def _rope_tables(pos2, invf):
    return pl.pallas_call(
        _rope_table_kernel,
        grid=(TOKENS // ROW_TILE,),
        in_specs=[_row_spec(1), _const_spec(invf.shape)],
        out_specs=[_row_spec(LANES)] * 3,
        out_shape=[jax.ShapeDtypeStruct((TOKENS, LANES), F32)] * 3,
        compiler_params=pltpu.CompilerParams(dimension_semantics=("parallel",)),
        name="rope_tables",
    )(pos2, invf)


def _bias_of(rel, tab_ref, head):
    b = jnp.full(rel.shape, tab_ref[head], F32)
    for j in range(1, REL_BUCKETS):
        b = jnp.where(rel >= _THRESH[j], tab_ref[j * BIAS_HEADS + head], b)
    return b


def _da_bias_kernel(tab_ref, out_ref):
    h = pl.program_id(0)
    i = lax.broadcasted_iota(jnp.int32, (ATT_TILE, ATT_TILE), 0)
    j = lax.broadcasted_iota(jnp.int32, (ATT_TILE, ATT_TILE), 1)
    rel = i - j
    out_ref[0, 0] = jnp.where(rel >= 0, _bias_of(rel, tab_ref, h) * LOG2E, NEG)
    out_ref[0, 1] = _bias_of(rel + ATT_TILE, tab_ref, h) * LOG2E


def _sw_bias_kernel(tab_ref, out_ref):
    t = pl.program_id(0)
    g = pl.program_id(1)
    i = lax.broadcasted_iota(jnp.int32, (SW_SUB, SW_KEYS), 0)
    j = lax.broadcasted_iota(jnp.int32, (SW_SUB, SW_KEYS), 1)
    rel = i - j + t * SW_WINDOW
    ok = (rel >= 0) & (rel < SW_WINDOW)
    for a in range(SW_GROUP):
        head = DA_HEADS + g * SW_GROUP + a
        out_ref[0, 0, a * SW_SUB:(a + 1) * SW_SUB, :] = jnp.where(
            ok, _bias_of(rel, tab_ref, head) * LOG2E, NEG)


def _bias_tiles(rel_bias):
    tab = rel_bias.astype(F32).reshape(REL_BUCKETS * BIAS_HEADS)
    smem = pl.BlockSpec(memory_space=pltpu.SMEM)
    da = pl.pallas_call(
        _da_bias_kernel,
        grid=(DA_HEADS,),
        in_specs=[smem],
        out_specs=pl.BlockSpec((1, 2, ATT_TILE, ATT_TILE), lambda h: (h, 0, 0, 0)),
        out_shape=jax.ShapeDtypeStruct((DA_HEADS, 2, ATT_TILE, ATT_TILE), F32),
        name="da_bias_tiles",
    )(tab)
    sw = pl.pallas_call(
        _sw_bias_kernel,
        grid=(2, SW_KV_HEADS),
        in_specs=[smem],
        out_specs=pl.BlockSpec((1, 1, SW_ROWS, SW_KEYS), lambda t, g: (t, g, 0, 0)),
        out_shape=jax.ShapeDtypeStruct((2, SW_KV_HEADS, SW_ROWS, SW_KEYS), F32),
        name="sw_bias_tiles",
    )(tab)
    return tab, da, sw


def _inproj_kernel(x_ref, cos_ref, sin_a_ref, sin_b_ref, npre_ref, wlo_ref, whi_ref, qn_ref,
                   wqb_ref, kvn_ref, wk_ref, wv_ref, aq_ref, ak_ref, av_ref, cq_ref, ck_ref,
                   cv_ref, mq_ref, mk_ref, mv_ref):
    h = _rms(x_ref[...], npre_ref[0]).astype(BF16)
    lo = lambda c0, c1: _dot(h, wlo_ref[0, :, c0:c1])
    hi = lambda c0, c1: _dot(h, whi_ref[0, :, c0:c1])
    cos = cos_ref[...]
    sin_a = sin_a_ref[...]
    sin_b = sin_b_ref[...]
    lane = lax.broadcasted_iota(jnp.int32, (1, LANES), 1)

    def rope(t):
        return (t * cos + pltpu.roll(t, LANES - MLA_ROPE // 2, 1) * sin_a
                + pltpu.roll(t, MLA_ROPE // 2, 1) * sin_b)

    bq = lo(_OFF[4], _OFF[5])
    bkv_kr = lo(_OFF[5], _LO_END)
    bkv = bkv_kr[:, :MLA_KV_RANK]
    kr = jnp.where(lane < MLA_ROPE, bkv_kr[:, MLA_KV_RANK:], 0.0)
    kr = rope(pltpu.roll(kr, MLA_NOPE, 1))
    qm = _dot(_rms(bq, qn_ref[0]).astype(BF16), wqb_ref[0])
    kvn = _rms(bkv, kvn_ref[0]).astype(BF16)
    km = _dot(kvn, wk_ref[0])
    mv_ref[...] = _dot(kvn, wv_ref[0]).astype(BF16)
    for hh in range(MLA_HEADS):
        sl = slice(hh * LANES, (hh + 1) * LANES)
        mq_ref[:, sl] = (rope(qm[:, sl]) * MLA_SCALE).astype(BF16)
        mk_ref[:, sl] = (km[:, sl] + kr).astype(BF16)

    aq_ref[...] = (lo(_OFF[0], _OFF[1]) * DA_SCALE).astype(BF16)
    ak_ref[...] = lo(_OFF[1], _OFF[2]).astype(BF16)
    av_ref[...] = lo(_OFF[2], _OFF[3]).astype(BF16)
    cq_ref[...] = (hi(_hi(8), _hi(9)) * SW_SCALE).astype(BF16)
    ckv = hi(_hi(9), _hi(11))
    ck_ref[...] = ckv[:, :IN_SIZES[9]].astype(BF16)
    cv_ref[...] = ckv[:, IN_SIZES[9]:].astype(BF16)


def _inproj(l, x2, rope_tabs, npre, wlo, whi, qn, wqb, kvn, wk, wv):
    widths = (DA_QK, DA_QK, DA_WIDTH, SW_WIDTH, IN_SIZES[9], IN_SIZES[10], MLA_HEADS * LANES,
              MLA_HEADS * LANES, MLA_WIDTH)
    return pl.pallas_call(
        _inproj_kernel,
        grid=(TOKENS // ROW_TILE,),
        in_specs=[_row_spec(D_MODEL), _row_spec(LANES), _row_spec(LANES), _row_spec(LANES),
                  _layer_spec(npre, l), _layer_spec(wlo, l), _layer_spec(whi, l, cols=_HI_INPROJ),
                  _layer_spec(qn, l), _layer_spec(wqb, l), _layer_spec(kvn, l),
                  _layer_spec(wk, l), _layer_spec(wv, l)],
        out_specs=[_row_spec(w) for w in widths],
        out_shape=[jax.ShapeDtypeStruct((TOKENS, w), BF16) for w in widths],
        compiler_params=pltpu.CompilerParams(dimension_semantics=("parallel",)),
        name="inproj",
    )(x2, *rope_tabs, npre, wlo, whi, qn, wqb, kvn, wk, wv)


def _with_ones(v):
    return jnp.concatenate([v, jnp.ones(v.shape, v.dtype)], axis=1)


def _per_step(body):
    j = pl.program_id(2)
    for idx, tiles in enumerate(ATT_STEPS):
        def run(tiles=tiles):
            for n in tiles:
                body(n)
        if len(ATT_STEPS) == 1:
            run()
        else:
            pl.when(j == idx)(run)


def _da_kernel(tab_ref, lam_ref, subln_ref, q_ref, k_ref, v_ref, bias_ref, o_ref, *, lam_init):
    t = ATT_TILE
    h = pl.program_id(1)
    far_bias = tab_ref[(REL_BUCKETS - 1) * BIAS_HEADS + h] * LOG2E
    lane = lax.broadcasted_iota(jnp.int32, (1, LANES), 1)
    lo = lane < HEAD_DIM

    def body(n):
        q = q_ref[n * t:(n + 1) * t, :]
        zero = jnp.zeros_like(q)
        p = []
        for qh in (jnp.where(lo, q, zero), jnp.where(lo, zero, q)):
            s_d = _dot_nt(qh, k_ref[n * t:(n + 1) * t, :]) + bias_ref[0, 0]
            mx = jnp.max(s_d, axis=-1, keepdims=True)
            if n >= 1:
                s_p = _dot_nt(qh, k_ref[(n - 1) * t:n * t, :]) + bias_ref[0, 1]
                mx = jnp.maximum(mx, jnp.max(s_p, axis=-1, keepdims=True))
            if n >= 2:
                s_f = _dot_nt(qh, k_ref[0:(n - 1) * t, :])
                mx = jnp.maximum(mx, jnp.max(s_f, axis=-1, keepdims=True) + far_bias)
            parts = []
            if n >= 2:
                parts.append(jnp.exp2(s_f - (mx - far_bias)).astype(BF16))
            if n >= 1:
                parts.append(jnp.exp2(s_p - mx).astype(BF16))
            parts.append(jnp.exp2(s_d - mx).astype(BF16))
            p.append(jnp.concatenate(parts, axis=1) if len(parts) > 1 else parts[0])
        acc = _dot(jnp.concatenate(p, axis=0), _with_ones(v_ref[0:(n + 1) * t, :]))
        o = acc[:, :LANES] / acc[:, LANES:]
        o1, o2 = o[:t], o[t:]
        lf = lam_ref[0]
        lam = (jnp.exp(jnp.sum(lf[0:1] * lf[1:2], axis=-1, keepdims=True))
               - jnp.exp(jnp.sum(lf[2:3] * lf[3:4], axis=-1, keepdims=True)) + lam_init)
        o_ref[n * t:(n + 1) * t, :] = _rms(o1 - lam * o2, subln_ref[0]) * (1.0 - lam_init)

    _per_step(body)


def _da_attention(l, tab, lam_p, subln, aq, ak, av, da_bias):
    smem = pl.BlockSpec(memory_space=pltpu.SMEM)
    seq_slab = pl.BlockSpec((SEQ, LANES), lambda b, h, i: (b, h))
    return pl.pallas_call(
        functools.partial(_da_kernel, lam_init=0.8 - 0.6 * math.exp(-0.3 * l)),
        grid=(BATCH, DA_HEADS, len(ATT_STEPS)),
        in_specs=[smem, _layer_spec(lam_p, l), _layer_spec(subln, l), seq_slab, seq_slab,
                  seq_slab,
                  pl.BlockSpec((1, 2, ATT_TILE, ATT_TILE), lambda b, h, i: (h, 0, 0, 0))],
        out_specs=seq_slab,
        out_shape=jax.ShapeDtypeStruct((TOKENS, DA_WIDTH), F32),
        compiler_params=pltpu.CompilerParams(
            dimension_semantics=("parallel", "parallel", "arbitrary")),
        name="diff_attention",
    )(tab, lam_p, subln, aq, ak, av, da_bias)


def _mla_kernel(q_ref, k_ref, v_ref, o_ref):
    t = ATT_TILE
    lane = lax.broadcasted_iota(jnp.int32, (1, LANES), 1)
    row = lax.broadcasted_iota(jnp.int32, (t, t), 0)
    col = lax.broadcasted_iota(jnp.int32, (t, t), 1)

    def body(n):
        p = []
        for a in range(2):
            sl = slice(a * LANES, (a + 1) * LANES)
            q = q_ref[n * t:(n + 1) * t, sl]
            s_d = jnp.where(row >= col, _dot_nt(q, k_ref[n * t:(n + 1) * t, sl]), NEG)
            mx = jnp.max(s_d, axis=-1, keepdims=True)
            parts = []
            if n >= 1:
                s_f = _dot_nt(q, k_ref[0:n * t, sl])
                mx = jnp.maximum(mx, jnp.max(s_f, axis=-1, keepdims=True))
                parts.append(jnp.exp2(s_f - mx).astype(BF16))
            parts.append(jnp.exp2(s_d - mx).astype(BF16))
            p.append(jnp.concatenate(parts, axis=1) if len(parts) > 1 else parts[0])
        acc = _dot(jnp.concatenate(p, axis=0), _with_ones(v_ref[0:(n + 1) * t, :]))
        o = acc[:, :LANES] / acc[:, LANES:]
        o_ref[n * t:(n + 1) * t, :] = jnp.where(lane < MLA_V, o[:t], o[t:])

    _per_step(body)


def _mla_attention(mq, mk, mv):
    pair_slab = pl.BlockSpec((SEQ, 2 * LANES), lambda b, h, i: (b, h))
    seq_slab = pl.BlockSpec((SEQ, LANES), lambda b, h, i: (b, h))
    return pl.pallas_call(
        _mla_kernel,
        grid=(BATCH, MLA_HEADS // 2, len(ATT_STEPS)),
        in_specs=[pair_slab, pair_slab, seq_slab],
        out_specs=seq_slab,
        out_shape=jax.ShapeDtypeStruct((TOKENS, MLA_WIDTH), F32),
        compiler_params=pltpu.CompilerParams(
            dimension_semantics=("parallel", "parallel", "arbitrary")),
        name="mla_attention",
    )(mq, mk, mv)


def _sw_kernel(sink_ref, q_ref, k_ref, v_ref, bias_ref, o_ref, *, layer):
    qi = pl.program_id(1)
    lane = lax.broadcasted_iota(jnp.int32, (1, LANES), 1)
    lo = lane < HEAD_DIM
    head_of_row = lax.broadcasted_iota(jnp.int32, (SW_ROWS, 1), 0) // SW_SUB

    for sub in range(SW_SUBS):
        r0 = qi * SW_TILE + sub * SW_SUB
        first = r0 == 0
        key0 = pl.multiple_of(jnp.maximum(r0 - SW_WINDOW, 0), SW_WINDOW)
        k = k_ref[pl.ds(key0, SW_KEYS), :]
        v = v_ref[pl.ds(key0, SW_KEYS), :]
        k_sw = pltpu.roll(k, HEAD_DIM, 1)
        v_sw = pltpu.roll(v, HEAD_DIM, 1)
        variant = jnp.where(first, 0, 1)
        for g in range(SW_KV_HEADS):
            kg = jnp.where(lo, k, k_sw) if g == 0 else jnp.where(lo, k_sw, k)
            vg = jnp.where(lo, v, v_sw) if g == 0 else jnp.where(lo, v_sw, v)
            qs = []
            for slab in range(2):
                c0 = (2 * g + slab) * LANES
                q = q_ref[sub * SW_SUB:(sub + 1) * SW_SUB, c0:c0 + LANES]
                zero = jnp.zeros_like(q)
                qs += [jnp.where(lo, q, zero), jnp.where(lo, zero, q)]
            s = _dot_nt(jnp.concatenate(qs, axis=0), kg) + bias_ref[variant, g]
            sink = jnp.zeros((SW_ROWS, 1), F32)
            for a in range(SW_GROUP):
                sink_a = sink_ref[layer * SW_HEADS + g * SW_GROUP + a] * LOG2E
                sink = jnp.where(head_of_row == a, sink_a, sink)
            m = jnp.maximum(jnp.max(s, axis=-1, keepdims=True), sink)
            acc = _dot(jnp.exp2(s - m).astype(BF16), _with_ones(vg))
            o = acc[:, :LANES] / (acc[:, LANES:] + jnp.exp2(sink - m))
            for slab in range(2):
                c0 = (2 * g + slab) * LANES
                ra = 2 * slab * SW_SUB
                o_ref[sub * SW_SUB:(sub + 1) * SW_SUB, c0:c0 + LANES] = jnp.where(
                    lo, o[ra:ra + SW_SUB], o[ra + SW_SUB:ra + 2 * SW_SUB])


def _sw_attention(l, sinks, cq, ck, cv, sw_bias):
    smem = pl.BlockSpec(memory_space=pltpu.SMEM)
    kv_slab = pl.BlockSpec((SEQ, LANES), lambda b, i: (b, 0))
    q_tile = pl.BlockSpec((SW_TILE, SW_WIDTH), lambda b, i: (b * SW_NQ + i, 0))
    return pl.pallas_call(
        functools.partial(_sw_kernel, layer=l),
        grid=(BATCH, SW_NQ),
        in_specs=[smem, q_tile, kv_slab, kv_slab, _const_spec(sw_bias.shape)],
        out_specs=q_tile,
        out_shape=jax.ShapeDtypeStruct((TOKENS, SW_WIDTH), F32),
        compiler_params=pltpu.CompilerParams(dimension_semantics=("parallel", "parallel")),
        name="sw_attention",
    )(sinks, cq, ck, cv, sw_bias)


def _merge_kernel(x_ref, p_ref, oa_ref, ob_ref, oc_ref, npre_ref, npost_ref, waz_ref, whi_ref,
                  wbra_ref, wbrb_ref, wbrc_ref, wout_ref, wpg_ref, wpp_ref, out_ref):
    x = x_ref[...]
    h = _rms(x, npre_ref[0]).astype(BF16)
    z_weights = (waz_ref[0], whi_ref[0, :, _hi(7):_hi(8)], whi_ref[0, :, _hi(11):_hi(12)])
    y = None
    for i, (o_ref, wbr_ref) in enumerate(((oa_ref, wbra_ref), (ob_ref, wbrb_ref),
                                          (oc_ref, wbrc_ref))):
        z = _dot(h, z_weights[i])
        t = (o_ref[...] * (z * _sigmoid(z))).astype(BF16)
        br = _dot(t, wbr_ref[0])
        g0 = _hi(12) + i * D_MODEL
        g = _sigmoid(_dot(h, whi_ref[0, :, g0:g0 + D_MODEL]))
        y = g * br if y is None else y + g * br
    x1 = x + _rms(_dot(y.astype(BF16), wout_ref[0]), npost_ref[0])
    u = _dot(x1.astype(BF16), wpg_ref[0])
    pp = _dot(p_ref[0].astype(BF16), wpp_ref[0])
    out_ref[...] = x1 + _sigmoid(u) * pp


def _merge(l, x2, p3, oa, ob, oc, npre, npost, wlo, whi, wbra, wbrb, wbrc, wout, wpg, wpp):
    return pl.pallas_call(
        _merge_kernel,
        grid=(TOKENS // ROW_TILE,),
        in_specs=[_row_spec(D_MODEL),
                  pl.BlockSpec((1, ROW_TILE, PLE_DIM), lambda i: (l, i, 0)),
                  _row_spec(DA_WIDTH), _row_spec(MLA_WIDTH), _row_spec(SW_WIDTH),
                  _layer_spec(npre, l), _layer_spec(npost, l),
                  _layer_spec(wlo, l, cols=DA_WIDTH, col_block=_OFF[3] // DA_WIDTH),
                  _layer_spec(whi, l), _layer_spec(wbra, l), _layer_spec(wbrb, l),
                  _layer_spec(wbrc, l), _layer_spec(wout, l), _layer_spec(wpg, l),
                  _layer_spec(wpp, l)],
        out_specs=_row_spec(D_MODEL),
        out_shape=jax.ShapeDtypeStruct((TOKENS, D_MODEL), F32),
        compiler_params=pltpu.CompilerParams(dimension_semantics=("parallel",)),
        name="gated_merge",
    )(x2, p3, oa, ob, oc, npre, npost, wlo, whi, wbra, wbrb, wbrc, wout, wpg, wpp)


def _mla_weights(mla_w_qb, mla_w_kvb):
    wqb = mla_w_qb.reshape(DEPTH, MLA_Q_RANK, MLA_HEADS, MLA_NOPE + MLA_ROPE)
    wqb = jnp.pad(wqb, ((0, 0), (0, 0), (0, 0), (0, LANES - MLA_NOPE - MLA_ROPE)))
    wqb = wqb.reshape(DEPTH, MLA_Q_RANK, MLA_HEADS * LANES).astype(BF16)
    wkv = mla_w_kvb.reshape(DEPTH, MLA_KV_RANK, MLA_HEADS, MLA_NOPE + MLA_V)
    wk = jnp.pad(wkv[..., :MLA_NOPE], ((0, 0), (0, 0), (0, 0), (0, LANES - MLA_NOPE)))
    wk = wk.reshape(DEPTH, MLA_KV_RANK, MLA_HEADS * LANES).astype(BF16)
    wv = wkv[..., MLA_NOPE:].reshape(DEPTH, MLA_KV_RANK, MLA_WIDTH).astype(BF16)
    return wqb, wk, wv


def kernel(x, p, positions, rel_bias, norm_pre, norm_post, w_in, da_lambda, da_subln, mla_q_norm,
           mla_w_qb, mla_kv_norm, mla_w_kvb, sw_sinks, w_br_a, w_br_b, w_br_c, w_out, w_ple_gate,
           w_ple_proj):
    assert x.shape == (BATCH, SEQ, D_MODEL) and p.shape == (DEPTH, BATCH, SEQ, PLE_DIM)
    assert w_in.shape == (DEPTH, D_MODEL, _OFF[-1])
    x2 = x.reshape(TOKENS, D_MODEL).astype(F32)
    p3 = p.reshape(DEPTH, TOKENS, PLE_DIM).astype(F32)
    pos2 = positions.reshape(TOKENS, 1).astype(jnp.int32)
    half = MLA_ROPE // 2
    inv_freq = ROPE_THETA ** (-jnp.arange(half, dtype=F32) / half)
    invf = jnp.zeros((1, LANES), F32)
    invf = invf.at[0, MLA_NOPE:MLA_NOPE + half].set(inv_freq)
    invf = invf.at[0, MLA_NOPE + half:MLA_NOPE + 2 * half].set(inv_freq)
    rope_tabs = _rope_tables(pos2, invf)
    tab, da_bias, sw_bias = _bias_tiles(rel_bias)

    wlo = w_in[:, :, :_LO_END].astype(BF16)
    whi = w_in[:, :, _HI0:].astype(BF16)
    wqb, wk, wv = _mla_weights(mla_w_qb, mla_w_kvb)
    wbra, wbrb, wbrc = (w.astype(BF16) for w in (w_br_a, w_br_b, w_br_c))
    wout, wpg, wpp = (w.astype(BF16) for w in (w_out, w_ple_gate, w_ple_proj))
    vec = lambda a: a.astype(F32).reshape(DEPTH, 1, a.shape[-1])
    npre, npost, subln, qn, kvn = (vec(a) for a in (norm_pre, norm_post, da_subln, mla_q_norm,
                                                    mla_kv_norm))
    lam_p = da_lambda.astype(F32)
    sinks = sw_sinks.astype(F32).reshape(DEPTH * SW_HEADS)

    for l in range(DEPTH):
        aq, ak, av, cq, ck, cv, mq, mk, mv = _inproj(l, x2, rope_tabs, npre, wlo, whi, qn, wqb,
                                                     kvn, wk, wv)
        oa = _da_attention(l, tab, lam_p, subln, aq, ak, av, da_bias)
        ob = _mla_attention(mq, mk, mv)
        oc = _sw_attention(l, sinks, cq, ck, cv, sw_bias)
        x2 = _merge(l, x2, p3, oa, ob, oc, npre, npost, wlo, whi, wbra, wbrb, wbrc, wout, wpg,
                    wpp)
    return x2.reshape(BATCH, SEQ, D_MODEL)
```

```python
import functools
import math

import numpy as np
import jax
import jax.numpy as jnp
from jax import lax
from jax.experimental import pallas as pl
from jax.experimental.pallas import tpu as pltpu

D_MODEL = 1024
BATCH = 8
SEQ = 2048
DEPTH = 2
PLE_DIM = 256
HEAD_DIM = 64
DA_HEADS = 4
DA_V = 2 * HEAD_DIM
DA_QK = DA_HEADS * 2 * HEAD_DIM
DA_WIDTH = DA_HEADS * DA_V
MLA_HEADS = 8
MLA_NOPE = 64
MLA_ROPE = 32
MLA_V = 64
MLA_Q_RANK = 256
MLA_KV_RANK = 128
MLA_WIDTH = MLA_HEADS * MLA_V
SW_HEADS = 8
SW_KV_HEADS = 2
SW_WINDOW = 128
SW_WIDTH = SW_HEADS * HEAD_DIM
N_BRANCHES = 3
REL_BUCKETS = 32
REL_MAX_DIST = 128
BIAS_HEADS = DA_HEADS + SW_HEADS
ROPE_THETA = 10000.0
EPS = 1e-6
NEG = -1e30

IN_SIZES = (DA_QK, DA_QK, DA_WIDTH, DA_WIDTH,
            MLA_Q_RANK, MLA_KV_RANK, MLA_ROPE, MLA_WIDTH,
            SW_HEADS * HEAD_DIM, SW_KV_HEADS * HEAD_DIM, SW_KV_HEADS * HEAD_DIM, SW_WIDTH,
            N_BRANCHES * D_MODEL)
_OFF = tuple(int(c) for c in np.cumsum((0,) + IN_SIZES))

LANES = 128
TOKENS = BATCH * SEQ
ROW_TILE = 512
ATT_TILE = 512
ATT_NQ = SEQ // ATT_TILE
ATT_STEPS = tuple((n,) for n in range(ATT_NQ))
SW_GROUP = SW_HEADS // SW_KV_HEADS
SW_SUB = SW_WINDOW
SW_KEYS = SW_SUB + SW_WINDOW
SW_SUBS = 4
SW_TILE = SW_SUBS * SW_SUB
SW_NQ = SEQ // SW_TILE
SW_ROWS = SW_GROUP * SW_SUB
LOG2E = math.log2(math.e)
DA_SCALE = HEAD_DIM ** -0.5 * LOG2E
SW_SCALE = HEAD_DIM ** -0.5 * LOG2E
MLA_SCALE = (MLA_NOPE + MLA_ROPE) ** -0.5 * LOG2E
BF16 = jnp.bfloat16
F32 = jnp.float32

_LO_END = _OFF[6] + LANES
_HI0 = _OFF[7]
_HI_INPROJ = _OFF[11] - _HI0
assert _OFF[6] % LANES == 0 and all((_OFF[i] - _HI0) % LANES == 0 for i in range(7, 13))
assert _OFF[3] % DA_WIDTH == 0


def _hi(i):
    return _OFF[i] - _HI0


def _bucket_thresholds():
    max_exact = REL_BUCKETS // 2
    n = np.arange(0, REL_MAX_DIST + 1)
    nf = np.maximum(n, 1).astype(np.float64)
    large = max_exact + (np.log(nf / max_exact) / math.log(REL_MAX_DIST / max_exact)
                         * (REL_BUCKETS - max_exact)).astype(np.int64)
    bucket = np.where(n < max_exact, n, np.minimum(large, REL_BUCKETS - 1))
    return tuple(int(np.argmax(bucket >= j)) for j in range(REL_BUCKETS))


_THRESH = _bucket_thresholds()


def _dot(a, b):
    return jnp.dot(a, b, preferred_element_type=F32)


def _dot_nt(a, b):
    return lax.dot_general(a, b, (((1,), (1,)), ((), ())), preferred_element_type=F32)


def _rms(x, w):
    y = x * lax.rsqrt(jnp.mean(x * x, axis=-1, keepdims=True) + EPS)
    return y * w


def _sigmoid(x):
    return 1.0 / (1.0 + jnp.exp(-x))


def _const_spec(shape):
    nd = len(shape)
    return pl.BlockSpec(shape, lambda *_: (0,) * nd, pipeline_mode=pl.Buffered(1))


def _layer_spec(arr, l, cols=None, col_block=0):
    shape = (1,) + arr.shape[1:-1] + (arr.shape[-1] if cols is None else cols,)
    index = (l,) + (0,) * (len(shape) - 2) + (col_block,)
    return pl.BlockSpec(shape, lambda *_: index, pipeline_mode=pl.Buffered(1))


def _row_spec(width):
    return pl.BlockSpec((ROW_TILE, width), lambda i: (i, 0))


def _rope_table_kernel(pos_ref, invf_ref, cos_ref, sin_a_ref, sin_b_ref):
    ang = pos_ref[...].astype(F32) * invf_ref[...]
    sin = jnp.sin(ang)
    lane = lax.broadcasted_iota(jnp.int32, (1, LANES), 1)
    first = (lane >= MLA_NOPE) & (lane < MLA_NOPE + MLA_ROPE // 2)
    second = (lane >= MLA_NOPE + MLA_ROPE // 2) & (lane < MLA_NOPE + MLA_ROPE)
    cos_ref[...] = jnp.cos(ang)
    sin_a_ref[...] = jnp.where(first, -sin, 0.0)
    sin_b_ref[...] = jnp.where(second, sin, 0.0)


def _rope_tables(pos2, invf):
    return pl.pallas_call(
        _rope_table_kernel,
        grid=(TOKENS // ROW_TILE,),
        in_specs=[_row_spec(1), _const_spec(invf.shape)],
        out_specs=[_row_spec(LANES)] * 3,
        out_shape=[jax.ShapeDtypeStruct((TOKENS, LANES), F32)] * 3,
        compiler_params=pltpu.CompilerParams(dimension_semantics=("parallel",)),
        name="rope_tables",
    )(pos2, invf)


def _bias_of(rel, tab_ref, head):
    b = jnp.full(rel.shape, tab_ref[head], F32)
    for j in range(1, REL_BUCKETS):
        b = jnp.where(rel >= _THRESH[j], tab_ref[j * BIAS_HEADS + head], b)
    return b


def _da_bias_kernel(tab_ref, out_ref):
    h = pl.program_id(0)
    i = lax.broadcasted_iota(jnp.int32, (ATT_TILE, ATT_TILE), 0)
    j = lax.broadcasted_iota(jnp.int32, (ATT_TILE, ATT_TILE), 1)
    rel = i - j
    out_ref[0, 0] = jnp.where(rel >= 0, _bias_of(rel, tab_ref, h) * LOG2E, NEG)
    out_ref[0, 1] = _bias_of(rel + ATT_TILE, tab_ref, h) * LOG2E


def _sw_sink_col(first):
    return jnp.where(first, SW_KEYS - 1, 0)


def _sw_bias_kernel(tab_ref, sink_ref, out_ref):
    layer = pl.program_id(0)
    t = pl.program_id(1)
    g = pl.program_id(2)
    i = lax.broadcasted_iota(jnp.int32, (SW_SUB, SW_KEYS), 0)
    j = lax.broadcasted_iota(jnp.int32, (SW_SUB, SW_KEYS), 1)
    rel = i - j + t * SW_WINDOW
    ok = (rel >= 0) & (rel < SW_WINDOW)
    at_sink = j == _sw_sink_col(t == 0)
    for a in range(SW_GROUP):
        head = g * SW_GROUP + a
        bias = jnp.where(ok, _bias_of(rel, tab_ref, DA_HEADS + head) * LOG2E, NEG)
        sink = sink_ref[layer * SW_HEADS + head] * LOG2E
        out_ref[0, 0, 0, a * SW_SUB:(a + 1) * SW_SUB, :] = jnp.where(at_sink, sink, bias)


def _bias_tiles(rel_bias, sinks):
    tab = rel_bias.astype(F32).reshape(REL_BUCKETS * BIAS_HEADS)
    smem = pl.BlockSpec(memory_space=pltpu.SMEM)
    da = pl.pallas_call(
        _da_bias_kernel,
        grid=(DA_HEADS,),
        in_specs=[smem],
        out_specs=pl.BlockSpec((1, 2, ATT_TILE, ATT_TILE), lambda h: (h, 0, 0, 0)),
        out_shape=jax.ShapeDtypeStruct((DA_HEADS, 2, ATT_TILE, ATT_TILE), F32),
        name="da_bias_tiles",
    )(tab)
    sw = pl.pallas_call(
        _sw_bias_kernel,
        grid=(DEPTH, 2, SW_KV_HEADS),
        in_specs=[smem, smem],
        out_specs=pl.BlockSpec((1, 1, 1, SW_ROWS, SW_KEYS), lambda l, t, g: (l, t, g, 0, 0)),
        out_shape=jax.ShapeDtypeStruct((DEPTH, 2, SW_KV_HEADS, SW_ROWS, SW_KEYS), F32),
        name="sw_bias_tiles",
    )(tab, sinks)
    return tab, da, sw


def _inproj_kernel(x_ref, cos_ref, sin_a_ref, sin_b_ref, npre_ref, wlo_ref, whi_ref, qn_ref,
                   wqb_ref, kvn_ref, wk_ref, wv_ref, aq_ref, ak_ref, av_ref, cq_ref, ck_ref,
                   cv_ref, mq_ref, mk_ref, mv_ref):
    h = _rms(x_ref[...], npre_ref[0]).astype(BF16)
    lo = lambda c0, c1: _dot(h, wlo_ref[0, :, c0:c1])
    hi = lambda c0, c1: _dot(h, whi_ref[0, :, c0:c1])
    cos = cos_ref[...]
    sin_a = sin_a_ref[...]
    sin_b = sin_b_ref[...]
    lane = lax.broadcasted_iota(jnp.int32, (1, LANES), 1)

    def rope(t):
        return (t * cos + pltpu.roll(t, LANES - MLA_ROPE // 2, 1) * sin_a
                + pltpu.roll(t, MLA_ROPE // 2, 1) * sin_b)

    bq = lo(_OFF[4], _OFF[5])
    bkv_kr = lo(_OFF[5], _LO_END)
    bkv = bkv_kr[:, :MLA_KV_RANK]
    kr = jnp.where(lane < MLA_ROPE, bkv_kr[:, MLA_KV_RANK:], 0.0)
    kr = rope(pltpu.roll(kr, MLA_NOPE, 1))
    qm = _dot(_rms(bq, qn_ref[0]).astype(BF16), wqb_ref[0])
    kvn = _rms(bkv, kvn_ref[0]).astype(BF16)
    km = _dot(kvn, wk_ref[0])
    mv_ref[...] = _dot(kvn, wv_ref[0]).astype(BF16)
    for hh in range(MLA_HEADS):
        sl = slice(hh * LANES, (hh + 1) * LANES)
        mq_ref[:, sl] = (rope(qm[:, sl]) * MLA_SCALE).astype(BF16)
        mk_ref[:, sl] = (km[:, sl] + kr).astype(BF16)

    aq_ref[...] = (lo(_OFF[0], _OFF[1]) * DA_SCALE).astype(BF16)
    ak_ref[...] = lo(_OFF[1], _OFF[2]).astype(BF16)
    av_ref[...] = lo(_OFF[2], _OFF[3]).astype(BF16)
    cq_ref[...] = (hi(_hi(8), _hi(9)) * SW_SCALE).astype(BF16)
    ckv = hi(_hi(9), _hi(11))
    ck_ref[...] = ckv[:, :IN_SIZES[9]].astype(BF16)
    cv_ref[...] = ckv[:, IN_SIZES[9]:].astype(BF16)


def _inproj(l, x2, rope_tabs, npre, wlo, whi, qn, wqb, kvn, wk, wv):
    widths = (DA_QK, DA_QK, DA_WIDTH, SW_WIDTH, IN_SIZES[9], IN_SIZES[10], MLA_HEADS * LANES,
              MLA_HEADS * LANES, MLA_WIDTH)
    return pl.pallas_call(
        _inproj_kernel,
        grid=(TOKENS // ROW_TILE,),
        in_specs=[_row_spec(D_MODEL), _row_spec(LANES), _row_spec(LANES), _row_spec(LANES),
                  _layer_spec(npre, l), _layer_spec(wlo, l), _layer_spec(whi, l, cols=_HI_INPROJ),
                  _layer_spec(qn, l), _layer_spec(wqb, l), _layer_spec(kvn, l),
                  _layer_spec(wk, l), _layer_spec(wv, l)],
        out_specs=[_row_spec(w) for w in widths],
        out_shape=[jax.ShapeDtypeStruct((TOKENS, w), BF16) for w in widths],
        compiler_params=pltpu.CompilerParams(dimension_semantics=("parallel",)),
        name="inproj",
    )(x2, *rope_tabs, npre, wlo, whi, qn, wqb, kvn, wk, wv)


def _with_ones(v):
    return jnp.concatenate([v, jnp.ones(v.shape, v.dtype)], axis=1)


def _per_step(body):
    j = pl.program_id(2)
    for idx, tiles in enumerate(ATT_STEPS):
        def run(tiles=tiles):
            for n in tiles:
                body(n)
        if len(ATT_STEPS) == 1:
            run()
        else:
            pl.when(j == idx)(run)


def _da_kernel(tab_ref, lam_ref, subln_ref, q_ref, k_ref, v_ref, bias_ref, o_ref, *, lam_init):
    t = ATT_TILE
    h = pl.program_id(1)
    far_bias = tab_ref[(REL_BUCKETS - 1) * BIAS_HEADS + h] * LOG2E
    lane = lax.broadcasted_iota(jnp.int32, (1, LANES), 1)
    lo = lane < HEAD_DIM

    def body(n):
        q = q_ref[n * t:(n + 1) * t, :]
        zero = jnp.zeros_like(q)
        p = []
        for qh in (jnp.where(lo, q, zero), jnp.where(lo, zero, q)):
            s_d = _dot_nt(qh, k_ref[n * t:(n + 1) * t, :]) + bias_ref[0, 0]
            mx = jnp.max(s_d, axis=-1, keepdims=True)
            if n >= 1:
                s_p = _dot_nt(qh, k_ref[(n - 1) * t:n * t, :]) + bias_ref[0, 1]
                mx = jnp.maximum(mx, jnp.max(s_p, axis=-1, keepdims=True))
            if n >= 2:
                s_f = _dot_nt(qh, k_ref[0:(n - 1) * t, :])
                mx = jnp.maximum(mx, jnp.max(s_f, axis=-1, keepdims=True) + far_bias)
            parts = []
            if n >= 2:
                parts.append(jnp.exp2(s_f - (mx - far_bias)).astype(BF16))
            if n >= 1:
                parts.append(jnp.exp2(s_p - mx).astype(BF16))
            parts.append(jnp.exp2(s_d - mx).astype(BF16))
            p.append(jnp.concatenate(parts, axis=1) if len(parts) > 1 else parts[0])
        acc = _dot(jnp.concatenate(p, axis=0), _with_ones(v_ref[0:(n + 1) * t, :]))
        o = acc[:, :LANES] / acc[:, LANES:]
        o1, o2 = o[:t], o[t:]
        lf = lam_ref[0]
        lam = (jnp.exp(jnp.sum(lf[0:1] * lf[1:2], axis=-1, keepdims=True))
               - jnp.exp(jnp.sum(lf[2:3] * lf[3:4], axis=-1, keepdims=True)) + lam_init)
        o_ref[n * t:(n + 1) * t, :] = _rms(o1 - lam * o2, subln_ref[0]) * (1.0 - lam_init)

    _per_step(body)


def _da_attention(l, tab, lam_p, subln, aq, ak, av, da_bias):
    smem = pl.BlockSpec(memory_space=pltpu.SMEM)
    seq_slab = pl.BlockSpec((SEQ, LANES), lambda b, h, i: (b, h))
    return pl.pallas_call(
        functools.partial(_da_kernel, lam_init=0.8 - 0.6 * math.exp(-0.3 * l)),
        grid=(BATCH, DA_HEADS, len(ATT_STEPS)),
        in_specs=[smem, _layer_spec(lam_p, l), _layer_spec(subln, l), seq_slab, seq_slab,
                  seq_slab,
                  pl.BlockSpec((1, 2, ATT_TILE, ATT_TILE), lambda b, h, i: (h, 0, 0, 0))],
        out_specs=seq_slab,
        out_shape=jax.ShapeDtypeStruct((TOKENS, DA_WIDTH), F32),
        compiler_params=pltpu.CompilerParams(
            dimension_semantics=("parallel", "parallel", "arbitrary")),
        name="diff_attention",
    )(tab, lam_p, subln, aq, ak, av, da_bias)


def _mla_kernel(q_ref, k_ref, v_ref, o_ref):
    t = ATT_TILE
    lane = lax.broadcasted_iota(jnp.int32, (1, LANES), 1)
    row = lax.broadcasted_iota(jnp.int32, (t, t), 0)
    col = lax.broadcasted_iota(jnp.int32, (t, t), 1)

    def body(n):
        p = []
        for a in range(2):
            sl = slice(a * LANES, (a + 1) * LANES)
            q = q_ref[n * t:(n + 1) * t, sl]
            s_d = jnp.where(row >= col, _dot_nt(q, k_ref[n * t:(n + 1) * t, sl]), NEG)
            mx = jnp.max(s_d, axis=-1, keepdims=True)
            parts = []
            if n >= 1:
                s_f = _dot_nt(q, k_ref[0:n * t, sl])
                mx = jnp.maximum(mx, jnp.max(s_f, axis=-1, keepdims=True))
                parts.append(jnp.exp2(s_f - mx).astype(BF16))
            parts.append(jnp.exp2(s_d - mx).astype(BF16))
            p.append(jnp.concatenate(parts, axis=1) if len(parts) > 1 else parts[0])
        acc = _dot(jnp.concatenate(p, axis=0), _with_ones(v_ref[0:(n + 1) * t, :]))
        o = acc[:, :LANES] / acc[:, LANES:]
        o_ref[n * t:(n + 1) * t, :] = jnp.where(lane < MLA_V, o[:t], o[t:])

    _per_step(body)


def _mla_attention(mq, mk, mv):
    pair_slab = pl.BlockSpec((SEQ, 2 * LANES), lambda b, h, i: (b, h))
    seq_slab = pl.BlockSpec((SEQ, LANES), lambda b, h, i: (b, h))
    return pl.pallas_call(
        _mla_kernel,
        grid=(BATCH, MLA_HEADS // 2, len(ATT_STEPS)),
        in_specs=[pair_slab, pair_slab, seq_slab],
        out_specs=seq_slab,
        out_shape=jax.ShapeDtypeStruct((TOKENS, MLA_WIDTH), F32),
        compiler_params=pltpu.CompilerParams(
            dimension_semantics=("parallel", "parallel", "arbitrary")),
        name="mla_attention",
    )(mq, mk, mv)


def _sw_kernel(q_ref, k_ref, v_ref, bias_ref, o_ref):
    qi = pl.program_id(1)
    lane = lax.broadcasted_iota(jnp.int32, (1, LANES), 1)
    lo = lane < HEAD_DIM
    key_idx = lax.broadcasted_iota(jnp.int32, (SW_KEYS, 1), 0)

    for sub in range(SW_SUBS):
        r0 = qi * SW_TILE + sub * SW_SUB
        first = r0 == 0
        key0 = pl.multiple_of(jnp.maximum(r0 - SW_WINDOW, 0), SW_WINDOW)
        is_sink = key_idx == _sw_sink_col(first)
        k = k_ref[pl.ds(key0, SW_KEYS), :]
        v = v_ref[pl.ds(key0, SW_KEYS), :]
        k = jnp.where(is_sink, jnp.zeros_like(k), k)
        v = jnp.where(is_sink, jnp.zeros_like(v), v)
        k_sw = pltpu.roll(k, HEAD_DIM, 1)
        v_sw = pltpu.roll(v, HEAD_DIM, 1)
        variant = jnp.where(first, 0, 1)
        for g in range(SW_KV_HEADS):
            kg = jnp.where(lo, k, k_sw) if g == 0 else jnp.where(lo, k_sw, k)
            vg = jnp.where(lo, v, v_sw) if g == 0 else jnp.where(lo, v_sw, v)
            qs = []
            for slab in range(2):
                c0 = (2 * g + slab) * LANES
                q = q_ref[sub * SW_SUB:(sub + 1) * SW_SUB, c0:c0 + LANES]
                zero = jnp.zeros_like(q)
                qs += [jnp.where(lo, q, zero), jnp.where(lo, zero, q)]
            s = _dot_nt(jnp.concatenate(qs, axis=0), kg) + bias_ref[0, variant, g]
            m = jnp.max(s, axis=-1, keepdims=True)
            acc = _dot(jnp.exp2(s - m).astype(BF16), _with_ones(vg))
            o = acc[:, :LANES] / acc[:, LANES:]
            for slab in range(2):
                c0 = (2 * g + slab) * LANES
                ra = 2 * slab * SW_SUB
                o_ref[sub * SW_SUB:(sub + 1) * SW_SUB, c0:c0 + LANES] = jnp.where(
                    lo, o[ra:ra + SW_SUB], o[ra + SW_SUB:ra + 2 * SW_SUB])


def _sw_attention(l, cq, ck, cv, sw_bias):
    kv_slab = pl.BlockSpec((SEQ, LANES), lambda b, i: (b, 0))
    q_tile = pl.BlockSpec((SW_TILE, SW_WIDTH), lambda b, i: (b * SW_NQ + i, 0))
    return pl.pallas_call(
        _sw_kernel,
        grid=(BATCH, SW_NQ),
        in_specs=[q_tile, kv_slab, kv_slab, _layer_spec(sw_bias, l)],
        out_specs=q_tile,
        out_shape=jax.ShapeDtypeStruct((TOKENS, SW_WIDTH), F32),
        compiler_params=pltpu.CompilerParams(dimension_semantics=("parallel", "parallel")),
        name="sw_attention",
    )(cq, ck, cv, sw_bias)


def _merge_kernel(x_ref, p_ref, oa_ref, ob_ref, oc_ref, npre_ref, npost_ref, waz_ref, whi_ref,
                  wbra_ref, wbrb_ref, wbrc_ref, wout_ref, wpg_ref, wpp_ref, out_ref):
    x = x_ref[...]
    h = _rms(x, npre_ref[0]).astype(BF16)
    z_weights = (waz_ref[0], whi_ref[0, :, _hi(7):_hi(8)], whi_ref[0, :, _hi(11):_hi(12)])
    y = None
    for i, (o_ref, wbr_ref) in enumerate(((oa_ref, wbra_ref), (ob_ref, wbrb_ref),
                                          (oc_ref, wbrc_ref))):
        z = _dot(h, z_weights[i])
        t = (o_ref[...] * (z * _sigmoid(z))).astype(BF16)
        br = _dot(t, wbr_ref[0])
        g0 = _hi(12) + i * D_MODEL
        g = _sigmoid(_dot(h, whi_ref[0, :, g0:g0 + D_MODEL]))
        y = g * br if y is None else y + g * br
    x1 = x + _rms(_dot(y.astype(BF16), wout_ref[0]), npost_ref[0])
    u = _dot(x1.astype(BF16), wpg_ref[0])
    pp = _dot(p_ref[0].astype(BF16), wpp_ref[0])
    out_ref[...] = x1 + _sigmoid(u) * pp


def _merge(l, x2, p3, oa, ob, oc, npre, npost, wlo, whi, wbra, wbrb, wbrc, wout, wpg, wpp):
    return pl.pallas_call(
        _merge_kernel,
        grid=(TOKENS // ROW_TILE,),
        in_specs=[_row_spec(D_MODEL),
                  pl.BlockSpec((1, ROW_TILE, PLE_DIM), lambda i: (l, i, 0)),
                  _row_spec(DA_WIDTH), _row_spec(MLA_WIDTH), _row_spec(SW_WIDTH),
                  _layer_spec(npre, l), _layer_spec(npost, l),
                  _layer_spec(wlo, l, cols=DA_WIDTH, col_block=_OFF[3] // DA_WIDTH),
                  _layer_spec(whi, l), _layer_spec(wbra, l), _layer_spec(wbrb, l),
                  _layer_spec(wbrc, l), _layer_spec(wout, l), _layer_spec(wpg, l),
                  _layer_spec(wpp, l)],
        out_specs=_row_spec(D_MODEL),
        out_shape=jax.ShapeDtypeStruct((TOKENS, D_MODEL), F32),
        compiler_params=pltpu.CompilerParams(dimension_semantics=("parallel",)),
        name="gated_merge",
    )(x2, p3, oa, ob, oc, npre, npost, wlo, whi, wbra, wbrb, wbrc, wout, wpg, wpp)


def _mla_weights(mla_w_qb, mla_w_kvb):
    wqb = mla_w_qb.reshape(DEPTH, MLA_Q_RANK, MLA_HEADS, MLA_NOPE + MLA_ROPE)
    wqb = jnp.pad(wqb, ((0, 0), (0, 0), (0, 0), (0, LANES - MLA_NOPE - MLA_ROPE)))
    wqb = wqb.reshape(DEPTH, MLA_Q_RANK, MLA_HEADS * LANES).astype(BF16)
    wkv = mla_w_kvb.reshape(DEPTH, MLA_KV_RANK, MLA_HEADS, MLA_NOPE + MLA_V)
    wk = jnp.pad(wkv[..., :MLA_NOPE], ((0, 0), (0, 0), (0, 0), (0, LANES - MLA_NOPE)))
    wk = wk.reshape(DEPTH, MLA_KV_RANK, MLA_HEADS * LANES).astype(BF16)
    wv = wkv[..., MLA_NOPE:].reshape(DEPTH, MLA_KV_RANK, MLA_WIDTH).astype(BF16)
    return wqb, wk, wv


def kernel(x, p, positions, rel_bias, norm_pre, norm_post, w_in, da_lambda, da_subln, mla_q_norm,
           mla_w_qb, mla_kv_norm, mla_w_kvb, sw_sinks, w_br_a, w_br_b, w_br_c, w_out, w_ple_gate,
           w_ple_proj):
    assert x.shape == (BATCH, SEQ, D_MODEL) and p.shape == (DEPTH, BATCH, SEQ, PLE_DIM)
    assert w_in.shape == (DEPTH, D_MODEL, _OFF[-1])
    x2 = x.reshape(TOKENS, D_MODEL).astype(F32)
    p3 = p.reshape(DEPTH, TOKENS, PLE_DIM).astype(F32)
    pos2 = positions.reshape(TOKENS, 1).astype(jnp.int32)
    half = MLA_ROPE // 2
    inv_freq = ROPE_THETA ** (-jnp.arange(half, dtype=F32) / half)
    invf = jnp.zeros((1, LANES), F32)
    invf = invf.at[0, MLA_NOPE:MLA_NOPE + half].set(inv_freq)
    invf = invf.at[0, MLA_NOPE + half:MLA_NOPE + 2 * half].set(inv_freq)
    rope_tabs = _rope_tables(pos2, invf)
    sinks = sw_sinks.astype(F32).reshape(DEPTH * SW_HEADS)
    tab, da_bias, sw_bias = _bias_tiles(rel_bias, sinks)

    wlo = w_in[:, :, :_LO_END].astype(BF16)
    whi = w_in[:, :, _HI0:].astype(BF16)
    wqb, wk, wv = _mla_weights(mla_w_qb, mla_w_kvb)
    wbra, wbrb, wbrc = (w.astype(BF16) for w in (w_br_a, w_br_b, w_br_c))
    wout, wpg, wpp = (w.astype(BF16) for w in (w_out, w_ple_gate, w_ple_proj))
    vec = lambda a: a.astype(F32).reshape(DEPTH, 1, a.shape[-1])
    npre, npost, subln, qn, kvn = (vec(a) for a in (norm_pre, norm_post, da_subln, mla_q_norm,
                                                    mla_kv_norm))
    lam_p = da_lambda.astype(F32)

    for l in range(DEPTH):
        aq, ak, av, cq, ck, cv, mq, mk, mv = _inproj(l, x2, rope_tabs, npre, wlo, whi, qn, wqb,
                                                     kvn, wk, wv)
        oa = _da_attention(l, tab, lam_p, subln, aq, ak, av, da_bias)
        ob = _mla_attention(mq, mk, mv)
        oc = _sw_attention(l, cq, ck, cv, sw_bias)
        x2 = _merge(l, x2, p3, oa, ob, oc, npre, npost, wlo, whi, wbra, wbrb, wbrc, wout, wpg,
                    wpp)
    return x2.reshape(BATCH, SEQ, D_MODEL)
```
